```python
import jax, jax.numpy as jnp
from jax import lax
import numpy as np

D_MODEL = 1024
BATCH = 2
SEQ = 8192
DEPTH = 1

POOL_WINDOWS = (2, 4, 8, 16)
N_POOL_GROUPS = len(POOL_WINDOWS)
D_POOL = D_MODEL
POOL_GW = D_POOL // N_POOL_GROUPS
POOL_OUT_GW = D_MODEL // N_POOL_GROUPS
D_CONV = D_MODEL
CONV_K = 3
D_IN = D_POOL + 3 * D_CONV + 2 * D_MODEL
D_FF = 2816
FFN_K = 3
N_MOD = 6
EPS = 1e-6

kernel_name = "hybrid_pool_shortconv_convffn_block"


def rmsnorm(x, g):
    xf = x.astype(jnp.float32)
    y = xf * lax.rsqrt(jnp.mean(xf * xf, axis=-1, keepdims=True) + EPS)
    return (y * g.astype(jnp.float32)).astype(x.dtype)


def causal_dwconv(x, w, b):
    k = w.shape[0]
    s = x.shape[1]
    xp = jnp.pad(x, ((0, 0), (k - 1, 0), (0, 0)))
    y = b
    for i in range(k):
        y = y + w[i] * xp[:, i:i + s]
    return y


def causal_multiscale_pool(u):
    bsz, s, _ = u.shape
    ug = u.reshape(bsz, s, N_POOL_GROUPS, POOL_GW)
    cs = jnp.cumsum(ug.astype(jnp.float32), axis=1)
    cs0 = jnp.pad(cs, ((0, 0), (1, 0), (0, 0), (0, 0)))
    t1 = jnp.arange(1, s + 1, dtype=jnp.float32)
    outs = []
    for g, w in enumerate(POOL_WINDOWS):
        upper = cs0[:, 1:, g]
        lower = jnp.pad(cs0[:, :s + 1 - w, g], ((0, 0), (w - 1, 0), (0, 0)))
        cnt = jnp.minimum(t1, float(w))[None, :, None]
        outs.append((upper - lower) / cnt)
    pooled = jnp.stack(outs, axis=2).astype(u.dtype)
    return pooled - ug


def setup_inputs(seed: int = 0) -> dict:
    key = jax.random.key(seed)
    ks = jax.random.split(key, 20)
    L, D = DEPTH, D_MODEL
    nrm = lambda k, shp, fan: jax.random.normal(k, shp, jnp.float32) * (fan ** -0.5)
    gain = lambda k, n: 1.0 + 0.05 * jax.random.normal(k, (L, n), jnp.float32)
    return {
        "x": jax.random.normal(ks[0], (BATCH, SEQ, D), jnp.float32),
        "c": jax.random.normal(ks[1], (BATCH, D), jnp.float32),
        "g_pre_mix": gain(ks[2], D),
        "g_post_mix": gain(ks[3], D),
        "g_pre_ffn": gain(ks[4], D),
        "g_post_ffn": gain(ks[5], D),
        "w_ada": 0.5 * nrm(ks[6], (L, D, N_MOD * D), D),
        "b_ada": 0.01 * jax.random.normal(ks[7], (L, N_MOD * D), jnp.float32),
        "w_in": nrm(ks[8], (L, D, D_IN), D),
        "w_pool": nrm(ks[9], (L, N_POOL_GROUPS, POOL_GW, POOL_OUT_GW), POOL_GW),
        "pool_scale": gain(ks[10], D),
        "conv_w": nrm(ks[11], (L, CONV_K, D_CONV), CONV_K),
        "conv_b": 0.01 * jax.random.normal(ks[12], (L, D_CONV), jnp.float32),
        "w_bout": nrm(ks[13], (L, D_CONV, D), D_CONV),
        "w_o": nrm(ks[14], (L, D, D), D),
        "w_up": nrm(ks[15], (L, D, 2 * D_FF), D),
        "ffn_conv_w": nrm(ks[16], (L, FFN_K, 2 * D_FF), FFN_K),
        "ffn_conv_b": 0.01 * jax.random.normal(ks[17], (L, 2 * D_FF), jnp.float32),
        "w_down": nrm(ks[18], (L, D_FF, D), D_FF),
    }


def reference(x, c, g_pre_mix, g_post_mix, g_pre_ffn, g_post_ffn, w_ada, b_ada, w_in, w_pool,
              pool_scale, conv_w, conv_b, w_bout, w_o, w_up, ffn_conv_w, ffn_conv_b, w_down):
    bsz, s, d = x.shape
    for l in range(DEPTH):
        mod = c @ w_ada[l] + b_ada[l]
        sh1, sc1, gt1, sh2, sc2, gt2 = [m[:, None, :] for m in jnp.split(mod, N_MOD, axis=-1)]

        h = rmsnorm(x, g_pre_mix[l]) * (1.0 + sc1) + sh1
        proj = h @ w_in[l]
        u_pool, u_x, u_b, u_c, z_a, z_b = jnp.split(
            proj, np.cumsum([D_POOL, D_CONV, D_CONV, D_CONV, D_MODEL])[:].tolist(), axis=-1)

        pg = causal_multiscale_pool(u_pool)
        y_a = jnp.einsum('bsgc,gcd->bsgd', pg, w_pool[l]).reshape(bsz, s, d) * pool_scale[l]

        y_b = (u_b * causal_dwconv(u_c * u_x, conv_w[l], conv_b[l])) @ w_bout[l]

        merged = jax.nn.sigmoid(z_a) * y_a + jax.nn.sigmoid(z_b) * y_b
        x = x + gt1 * rmsnorm(merged @ w_o[l], g_post_mix[l])

        h = rmsnorm(x, g_pre_ffn[l]) * (1.0 + sc2) + sh2
        up = causal_dwconv(h @ w_up[l], ffn_conv_w[l], ffn_conv_b[l])
        gate, val = jnp.split(up, 2, axis=-1)
        ff = (jax.nn.gelu(gate, approximate=True) * val) @ w_down[l]
        x = x + gt2 * rmsnorm(ff, g_post_ffn[l])
    return x
```

```python
import functools

import jax
import jax.numpy as jnp
from jax import lax
from jax.experimental import pallas as pl
from jax.experimental.pallas import tpu as pltpu

POOL_WINDOWS = (2, 4, 8, 16)
N_MOD = 6
EPS = 1e-6

V7X_SUBLANES = 8
V7X_LANES = 128
V7X_VMEM_BYTES = 64 * 1024 * 1024

POOL_HALO = 16
CONV_HALO = 8

MIXER_TILE = 256
FFN_TILE = 256
ADA_COLS = 1536


def _vmem_limit(resident_bytes, streamed_bytes, temp_bytes):
    need = resident_bytes + 2 * streamed_bytes + temp_bytes
    assert need < V7X_VMEM_BYTES, need
    return int(need)


def _const_spec(shape):
    zeros = (0,) * len(shape)
    return pl.BlockSpec(shape, lambda *_: zeros, pipeline_mode=pl.Buffered(1))


def _rmsnorm(xf, g):
    return xf * lax.rsqrt(jnp.mean(xf * xf, axis=-1, keepdims=True) + EPS) * g


def _shift_rows(carry, cur, shifts):
    halo = carry.shape[0]
    ext = jnp.concatenate([carry, cur], axis=0)
    return [pltpu.roll(ext, k, axis=0)[halo:] for k in shifts]


def _dot(a, b):
    return jnp.dot(a, b, preferred_element_type=jnp.float32)


def _ada_kernel(c_ref, w_ref, b_ref, o_ref):
    o_ref[...] = _dot(c_ref[...].astype(jnp.bfloat16), w_ref[...].astype(jnp.bfloat16)) + b_ref[...]


def _ada(c, w_ada, b_ada):
    bsz, d = c.shape
    n = w_ada.shape[1]
    return pl.pallas_call(
        _ada_kernel,
        out_shape=jax.ShapeDtypeStruct((bsz, n), jnp.float32),
        grid=(n // ADA_COLS,),
        in_specs=[
            pl.BlockSpec((bsz, d), lambda j: (0, 0)),
            pl.BlockSpec((d, ADA_COLS), lambda j: (0, j)),
            pl.BlockSpec((1, ADA_COLS), lambda j: (0, j)),
        ],
        out_specs=pl.BlockSpec((bsz, ADA_COLS), lambda j: (0, j)),
        compiler_params=pltpu.CompilerParams(
            dimension_semantics=("arbitrary",),
            vmem_limit_bytes=_vmem_limit(0, d * ADA_COLS * 4 + 4 * bsz * d * 4, d * ADA_COLS * 2 + (1 << 20))),
        name="adaln",
    )(c, w_ada, b_ada.reshape(1, n))


def _mixer_kernel(x_ref, mod_ref, gpre_ref, gpost_ref, win_ref, wpool_ref, pscale_ref, convw_ref, convb_ref,
                  wbout_ref, wo_ref, o_ref, pool_carry, conv_carry, *, tile, d):
    s = pl.program_id(1)

    @pl.when(s == 0)
    def _():
        pool_carry[...] = jnp.zeros_like(pool_carry)
        conv_carry[...] = jnp.zeros_like(conv_carry)

    x = x_ref[0]
    mod = mod_ref[0]
    sh1, sc1, gt1 = mod[0:1], mod[1:2], mod[2:3]
    hb = (_rmsnorm(x, gpre_ref[...]) * (1.0 + sc1) + sh1).astype(jnp.bfloat16)

    def proj(i):
        return _dot(hb, win_ref[:, i * d:(i + 1) * d])

    u = proj(0)
    ext = jnp.concatenate([pool_carry[...], u], axis=0)
    pool_carry[...] = u[tile - POOL_HALO:, :]
    gw = d // len(POOL_WINDOWS)
    t1 = (s * tile + lax.broadcasted_iota(jnp.int32, (tile, V7X_LANES), 0) + 1).astype(jnp.float32)
    ya_parts = []
    for g, w in enumerate(POOL_WINDOWS):
        acc = ext[:, g * gw:(g + 1) * gw]
        k = 1
        while k < w:
            acc = acc + pltpu.roll(acc, k, axis=0)
            k *= 2
        inv = 1.0 / jnp.minimum(t1, float(w))
        inv = jnp.concatenate([inv] * (gw // V7X_LANES), axis=1)
        pooled = acc[POOL_HALO:, :] * inv - u[:, g * gw:(g + 1) * gw]
        ya_parts.append(_dot(pooled.astype(jnp.bfloat16), wpool_ref[g]))
    ya = jnp.concatenate(ya_parts, axis=1) * pscale_ref[...]

    v = proj(3) * proj(1)
    v1, v2 = _shift_rows(conv_carry[...], v, (1, 2))
    conv_carry[...] = v[tile - CONV_HALO:, :]
    cw = convw_ref[...]
    conv = convb_ref[...] + cw[0:1] * v2 + cw[1:2] * v1 + cw[2:3] * v
    yb = _dot((proj(2) * conv).astype(jnp.bfloat16), wbout_ref[...])

    merged = jax.nn.sigmoid(proj(4)) * ya + jax.nn.sigmoid(proj(5)) * yb
    r = _dot(merged.astype(jnp.bfloat16), wo_ref[...])
    o_ref[0] = x + gt1 * _rmsnorm(r, gpost_ref[...])


def _mixer(x, mod, g_pre, g_post, w_in, w_pool, pool_scale, conv_w, conv_b, w_bout, w_o):
    bsz, seq, d = x.shape
    tile = MIXER_TILE
    d_in = w_in.shape[1]
    weights = 2 * (w_in.size + w_pool.size + w_bout.size + w_o.size)
    tok = lambda b, s: (b, s, 0)
    return pl.pallas_call(
        functools.partial(_mixer_kernel, tile=tile, d=d),
        out_shape=jax.ShapeDtypeStruct(x.shape, x.dtype),
        grid=(bsz, seq // tile),
        in_specs=[
            pl.BlockSpec((1, tile, d), tok),
            pl.BlockSpec((1, N_MOD, d), lambda b, s: (b, 0, 0)),
            _const_spec((1, d)), _const_spec((1, d)),
            _const_spec(w_in.shape), _const_spec(w_pool.shape), _const_spec((1, d)),
            _const_spec(conv_w.shape), _const_spec((1, d)),
            _const_spec(w_bout.shape), _const_spec(w_o.shape),
        ],
        out_specs=pl.BlockSpec((1, tile, d), tok),
        scratch_shapes=[pltpu.VMEM((POOL_HALO, d), jnp.float32), pltpu.VMEM((CONV_HALO, d), jnp.float32)],
        compiler_params=pltpu.CompilerParams(
            dimension_semantics=("arbitrary", "arbitrary"),
            vmem_limit_bytes=_vmem_limit(weights, 2 * tile * d * 4, 3 * tile * d_in * 4)),
        name="mixer",
    )(x, mod, g_pre.reshape(1, d), g_post.reshape(1, d), w_in, w_pool, pool_scale.reshape(1, d),
      conv_w, conv_b.reshape(1, d), w_bout, w_o)


def _ffn_kernel(x_ref, mod_ref, gpre_ref, gpost_ref, wup_ref, cw_ref, cb_ref, wdown_ref, o_ref, up_carry,
                *, tile, f, chunk):
    s = pl.program_id(1)

    @pl.when(s == 0)
    def _():
        up_carry[...] = jnp.zeros_like(up_carry)

    x = x_ref[0]
    mod = mod_ref[0]
    sh2, sc2, gt2 = mod[3:4], mod[4:5], mod[5:6]
    hb = (_rmsnorm(x, gpre_ref[...]) * (1.0 + sc2) + sh2).astype(jnp.bfloat16)

    def conv_cols(lo):
        cols = slice(lo, lo + chunk)
        up = _dot(hb, wup_ref[:, cols])
        u1, u2 = _shift_rows(up_carry[:, cols], up, (1, 2))
        up_carry[:, cols] = up[tile - CONV_HALO:, :]
        cw = cw_ref[:, cols]
        return cb_ref[:, cols] + cw[0:1] * u2 + cw[1:2] * u1 + cw[2:3] * up

    ff = None
    for j in range(f // chunk):
        gate = conv_cols(j * chunk)
        val = conv_cols(f + j * chunk)
        act = (jax.nn.gelu(gate, approximate=True) * val).astype(jnp.bfloat16)
        part = _dot(act, wdown_ref[j * chunk:(j + 1) * chunk, :])
        ff = part if ff is None else ff + part
    o_ref[0] = x + gt2 * _rmsnorm(ff, gpost_ref[...])


def _ffn(x, mod, g_pre, g_post, w_up, conv_w, conv_b, w_down):
    bsz, seq, d = x.shape
    tile = FFN_TILE
    f = w_down.shape[0]
    chunk = f // 2
    weights = 2 * (w_up.size + w_down.size)
    tok = lambda b, s: (b, s, 0)
    return pl.pallas_call(
        functools.partial(_ffn_kernel, tile=tile, f=f, chunk=chunk),
        out_shape=jax.ShapeDtypeStruct(x.shape, x.dtype),
        grid=(bsz, seq // tile),
        in_specs=[
            pl.BlockSpec((1, tile, d), tok),
            pl.BlockSpec((1, N_MOD, d), lambda b, s: (b, 0, 0)),
            _const_spec((1, d)), _const_spec((1, d)),
            _const_spec(w_up.shape), _const_spec(conv_w.shape), _const_spec((1, 2 * f)),
            _const_spec(w_down.shape),
        ],
        out_specs=pl.BlockSpec((1, tile, d), tok),
        scratch_shapes=[pltpu.VMEM((CONV_HALO, 2 * f), jnp.float32)],
        compiler_params=pltpu.CompilerParams(
            dimension_semantics=("arbitrary", "arbitrary"),
            vmem_limit_bytes=_vmem_limit(weights, 2 * tile * d * 4, 4 * tile * 2 * f * 4)),
        name="convffn",
    )(x, mod, g_pre.reshape(1, d), g_post.reshape(1, d), w_up, conv_w, conv_b.reshape(1, 2 * f), w_down)


def kernel(x, c, g_pre_mix, g_post_mix, g_pre_ffn, g_post_ffn, w_ada, b_ada, w_in, w_pool, pool_scale, conv_w,
           conv_b, w_bout, w_o, w_up, ffn_conv_w, ffn_conv_b, w_down):
    bsz, _, d = x.shape
    bf = lambda w: w.astype(jnp.bfloat16)
    for l in range(w_ada.shape[0]):
        mod = _ada(c, w_ada[l], b_ada[l]).reshape(bsz, N_MOD, d)
        x = _mixer(x, mod, g_pre_mix[l], g_post_mix[l], bf(w_in[l]), bf(w_pool[l]), pool_scale[l], conv_w[l],
                   conv_b[l], bf(w_bout[l]), bf(w_o[l]))
        x = _ffn(x, mod, g_pre_ffn[l], g_post_ffn[l], bf(w_up[l]), ffn_conv_w[l], ffn_conv_b[l], bf(w_down[l]))
    return x
```

```python
import functools

import jax
import jax.numpy as jnp
from jax import lax
from jax.experimental import pallas as pl
from jax.experimental.pallas import tpu as pltpu

POOL_WINDOWS = (2, 4, 8, 16)
N_MOD = 6
EPS = 1e-6

V7X_SUBLANES = 8
V7X_LANES = 128
V7X_VMEM_BYTES = 64 * 1024 * 1024

POOL_HALO = 16
CONV_HALO = 8

MIXER_TILE = 256
FFN_TILE = 256
FFN_CHUNKS = (1536, 1280)
ADA_COLS = 1536


def _vmem_limit(resident_bytes, streamed_bytes, temp_bytes):
    need = resident_bytes + 2 * streamed_bytes + temp_bytes
    assert need < V7X_VMEM_BYTES, need
    return int(need)


def _const_spec(shape):
    zeros = (0,) * len(shape)
    return pl.BlockSpec(shape, lambda *_: zeros, pipeline_mode=pl.Buffered(1))


def _rmsnorm(xf, g):
    return xf * lax.rsqrt(jnp.mean(xf * xf, axis=-1, keepdims=True) + EPS) * g


def _put_slabs(buf, halo, val, first):
    for i in range(val.shape[1] // V7X_LANES):
        buf[first + i, halo:, :] = val[:, i * V7X_LANES:(i + 1) * V7X_LANES]


def _rows_back(buf, halo, tile, slabs, k):
    return jnp.concatenate([buf[j, halo - k:halo - k + tile, :] for j in slabs], axis=1)


def _keep_history(buf, halo, tile, slabs):
    for j in slabs:
        buf[j, :halo, :] = buf[j, tile:, :]


def _zero_history(buf, halo):
    buf[:, :halo, :] = jnp.zeros((buf.shape[0], halo, V7X_LANES), jnp.float32)


def _dot(a, b):
    return jnp.dot(a, b, preferred_element_type=jnp.float32)


def _ada_kernel(c_ref, w_ref, b_ref, o_ref):
    o_ref[...] = _dot(c_ref[...].astype(jnp.bfloat16), w_ref[...].astype(jnp.bfloat16)) + b_ref[...]


def _ada(c, w_ada, b_ada):
    bsz, d = c.shape
    n = w_ada.shape[1]
    return pl.pallas_call(
        _ada_kernel,
        out_shape=jax.ShapeDtypeStruct((bsz, n), jnp.float32),
        grid=(n // ADA_COLS,),
        in_specs=[
            pl.BlockSpec((bsz, d), lambda j: (0, 0)),
            pl.BlockSpec((d, ADA_COLS), lambda j: (0, j)),
            pl.BlockSpec((1, ADA_COLS), lambda j: (0, j)),
        ],
        out_specs=pl.BlockSpec((bsz, ADA_COLS), lambda j: (0, j)),
        compiler_params=pltpu.CompilerParams(
            dimension_semantics=("arbitrary",),
            vmem_limit_bytes=_vmem_limit(0, d * ADA_COLS * 4 + 4 * bsz * d * 4, d * ADA_COLS * 2 + (1 << 20))),
        name="adaln",
    )(c, w_ada, b_ada.reshape(1, n))


def _mixer_kernel(x_ref, mod_ref, gpre_ref, gpost_ref, win_ref, wpool_ref, pscale_ref, convw_ref, convb_ref,
                  wbout_ref, wo_ref, o_ref, pool_buf, conv_buf, *, tile, d):
    s = pl.program_id(1)
    n_slabs = d // V7X_LANES

    @pl.when(s == 0)
    def _():
        _zero_history(pool_buf, POOL_HALO)
        _zero_history(conv_buf, CONV_HALO)

    x = x_ref[0]
    mod = mod_ref[0]
    sh1, sc1, gt1 = mod[0:1], mod[1:2], mod[2:3]
    hb = (_rmsnorm(x, gpre_ref[...]) * (1.0 + sc1) + sh1).astype(jnp.bfloat16)

    def proj(i):
        return _dot(hb, win_ref[:, i * d:(i + 1) * d])

    u = proj(0)
    _put_slabs(pool_buf, POOL_HALO, u, 0)
    gw = d // len(POOL_WINDOWS)
    t1 = (s * tile + lax.broadcasted_iota(jnp.int32, (tile, V7X_LANES), 0) + 1).astype(jnp.float32)
    ya_parts = []
    for g, w in enumerate(POOL_WINDOWS):
        slabs = range(g * gw // V7X_LANES, (g + 1) * gw // V7X_LANES)
        ug = u[:, g * gw:(g + 1) * gw]
        acc = ug
        for k in range(1, w):
            acc = acc + _rows_back(pool_buf, POOL_HALO, tile, slabs, k)
        inv = 1.0 / jnp.minimum(t1, float(w))
        inv = jnp.concatenate([inv] * len(slabs), axis=1)
        ya_parts.append(_dot((acc * inv - ug).astype(jnp.bfloat16), wpool_ref[g]))
    _keep_history(pool_buf, POOL_HALO, tile, range(n_slabs))
    ya = jnp.concatenate(ya_parts, axis=1) * pscale_ref[...]

    v = proj(3) * proj(1)
    _put_slabs(conv_buf, CONV_HALO, v, 0)
    v1 = _rows_back(conv_buf, CONV_HALO, tile, range(n_slabs), 1)
    v2 = _rows_back(conv_buf, CONV_HALO, tile, range(n_slabs), 2)
    _keep_history(conv_buf, CONV_HALO, tile, range(n_slabs))
    cw = convw_ref[...]
    conv = convb_ref[...] + cw[0:1] * v2 + cw[1:2] * v1 + cw[2:3] * v
    yb = _dot((proj(2) * conv).astype(jnp.bfloat16), wbout_ref[...])

    merged = jax.nn.sigmoid(proj(4)) * ya + jax.nn.sigmoid(proj(5)) * yb
    r = _dot(merged.astype(jnp.bfloat16), wo_ref[...])
    o_ref[0] = x + gt1 * _rmsnorm(r, gpost_ref[...])


def _mixer(x, mod, g_pre, g_post, w_in, w_pool, pool_scale, conv_w, conv_b, w_bout, w_o):
    bsz, seq, d = x.shape
    tile = MIXER_TILE
    d_in = w_in.shape[1]
    weights = 2 * (w_in.size + w_pool.size + w_bout.size + w_o.size)
    tok = lambda b, s: (b, s, 0)
    return pl.pallas_call(
        functools.partial(_mixer_kernel, tile=tile, d=d),
        out_shape=jax.ShapeDtypeStruct(x.shape, x.dtype),
        grid=(bsz, seq // tile),
        in_specs=[
            pl.BlockSpec((1, tile, d), tok),
            pl.BlockSpec((1, N_MOD, d), lambda b, s: (b, 0, 0)),
            _const_spec((1, d)), _const_spec((1, d)),
            _const_spec(w_in.shape), _const_spec(w_pool.shape), _const_spec((1, d)),
            _const_spec(conv_w.shape), _const_spec((1, d)),
            _const_spec(w_bout.shape), _const_spec(w_o.shape),
        ],
        out_specs=pl.BlockSpec((1, tile, d), tok),
        scratch_shapes=[pltpu.VMEM((d // V7X_LANES, POOL_HALO + tile, V7X_LANES), jnp.float32),
                        pltpu.VMEM((d // V7X_LANES, CONV_HALO + tile, V7X_LANES), jnp.float32)],
        compiler_params=pltpu.CompilerParams(
            dimension_semantics=("arbitrary", "arbitrary"),
            vmem_limit_bytes=_vmem_limit(weights, 2 * tile * d * 4, 3 * tile * d_in * 4)),
        name="mixer",
    )(x, mod, g_pre.reshape(1, d), g_post.reshape(1, d), w_in, w_pool, pool_scale.reshape(1, d),
      conv_w, conv_b.reshape(1, d), w_bout, w_o)


def _ffn_kernel(x_ref, mod_ref, gpre_ref, gpost_ref, wup_ref, cw_ref, cb_ref, wdown_ref, o_ref, up_buf,
                *, tile, f, chunks):
    s = pl.program_id(1)

    @pl.when(s == 0)
    def _():
        _zero_history(up_buf, CONV_HALO)

    x = x_ref[0]
    mod = mod_ref[0]
    sh2, sc2, gt2 = mod[3:4], mod[4:5], mod[5:6]
    hb = (_rmsnorm(x, gpre_ref[...]) * (1.0 + sc2) + sh2).astype(jnp.bfloat16)

    def conv_cols(lo, n):
        cols = slice(lo, lo + n)
        up = _dot(hb, wup_ref[:, cols])
        slabs = range(lo // V7X_LANES, (lo + n) // V7X_LANES)
        _put_slabs(up_buf, CONV_HALO, up, slabs[0])
        u1 = _rows_back(up_buf, CONV_HALO, tile, slabs, 1)
        u2 = _rows_back(up_buf, CONV_HALO, tile, slabs, 2)
        _keep_history(up_buf, CONV_HALO, tile, slabs)
        cw = cw_ref[:, cols]
        return cb_ref[:, cols] + cw[0:1] * u2 + cw[1:2] * u1 + cw[2:3] * up

    ff = None
    for lo, n in chunks:
        gate = conv_cols(lo, n)
        val = conv_cols(f + lo, n)
        act = (jax.nn.gelu(gate, approximate=True) * val).astype(jnp.bfloat16)
        part = _dot(act, wdown_ref[lo:lo + n, :])
        ff = part if ff is None else ff + part
    o_ref[0] = x + gt2 * _rmsnorm(ff, gpost_ref[...])


def _ffn(x, mod, g_pre, g_post, w_up, conv_w, conv_b, w_down):
    bsz, seq, d = x.shape
    tile = FFN_TILE
    f = w_down.shape[0]
    assert sum(FFN_CHUNKS) == f
    chunks = tuple((sum(FFN_CHUNKS[:i]), n) for i, n in enumerate(FFN_CHUNKS))
    weights = 2 * (w_up.size + w_down.size)
    tok = lambda b, s: (b, s, 0)
    return pl.pallas_call(
        functools.partial(_ffn_kernel, tile=tile, f=f, chunks=chunks),
        out_shape=jax.ShapeDtypeStruct(x.shape, x.dtype),
        grid=(bsz, seq // tile),
        in_specs=[
            pl.BlockSpec((1, tile, d), tok),
            pl.BlockSpec((1, N_MOD, d), lambda b, s: (b, 0, 0)),
            _const_spec((1, d)), _const_spec((1, d)),
            _const_spec(w_up.shape), _const_spec(conv_w.shape), _const_spec((1, 2 * f)),
            _const_spec(w_down.shape),
        ],
        out_specs=pl.BlockSpec((1, tile, d), tok),
        scratch_shapes=[pltpu.VMEM((2 * f // V7X_LANES, CONV_HALO + tile, V7X_LANES), jnp.float32)],
        compiler_params=pltpu.CompilerParams(
            dimension_semantics=("arbitrary", "arbitrary"),
            vmem_limit_bytes=_vmem_limit(weights, 2 * tile * d * 4, 4 * tile * 2 * f * 4)),
        name="convffn",
    )(x, mod, g_pre.reshape(1, d), g_post.reshape(1, d), w_up, conv_w, conv_b.reshape(1, 2 * f), w_down)


def kernel(x, c, g_pre_mix, g_post_mix, g_pre_ffn, g_post_ffn, w_ada, b_ada, w_in, w_pool, pool_scale, conv_w,
           conv_b, w_bout, w_o, w_up, ffn_conv_w, ffn_conv_b, w_down):
    bsz, _, d = x.shape
    bf = lambda w: w.astype(jnp.bfloat16)
    for l in range(w_ada.shape[0]):
        mod = _ada(c, w_ada[l], b_ada[l]).reshape(bsz, N_MOD, d)
        x = _mixer(x, mod, g_pre_mix[l], g_post_mix[l], bf(w_in[l]), bf(w_pool[l]), pool_scale[l], conv_w[l],
                   conv_b[l], bf(w_bout[l]), bf(w_o[l]))
        x = _ffn(x, mod, g_pre_ffn[l], g_post_ffn[l], bf(w_up[l]), ffn_conv_w[l], ffn_conv_b[l], bf(w_down[l]))
    return x
```

```python
import functools

import jax
import jax.numpy as jnp
from jax import lax
from jax.experimental import pallas as pl
from jax.experimental.pallas import tpu as pltpu

POOL_WINDOWS = (2, 4, 8, 16)
N_MOD = 6
EPS = 1e-6

V7X_LANES = 128
V7X_BF16_SUBLANES = 16
V7X_VMEM_BYTES = 64 * 1024 * 1024

POOL_HALO = 16
CONV_HALO = 8

MIXER_TILE = 256
FFN_TILE = 256
FFN_CHUNKS = (1536, 1280)
ADA_COLS = 1536


def _vmem_limit(resident_bytes, streamed_bytes, temp_bytes):
    need = resident_bytes + 2 * streamed_bytes + temp_bytes
    assert need < V7X_VMEM_BYTES, need
    return int(need)


def _const_spec(shape):
    zeros = (0,) * len(shape)
    return pl.BlockSpec(shape, lambda *_: zeros, pipeline_mode=pl.Buffered(1))


def _rmsnorm(xf, g):
    return xf * lax.rsqrt(jnp.mean(xf * xf, axis=-1, keepdims=True) + EPS) * g


def _put_slabs(buf, halo, val, first):
    for i in range(val.shape[1] // V7X_LANES):
        buf[first + i, halo:, :] = val[:, i * V7X_LANES:(i + 1) * V7X_LANES]


def _rows_back(buf, halo, tile, slabs, k):
    return jnp.concatenate([buf[j, halo - k:halo - k + tile, :] for j in slabs], axis=1)


def _keep_history(buf, halo, tile, slabs):
    for j in slabs:
        buf[j, :halo, :] = buf[j, tile:, :]


def _zero_history(buf, halo):
    buf[:, :halo, :] = jnp.zeros((buf.shape[0], halo, V7X_LANES), jnp.float32)


def _dot(a, b):
    return jnp.dot(a, b, preferred_element_type=jnp.float32)


def _row_blocks(rows, steps):
    for n in range(steps, 0, -1):
        if rows % n == 0 and (rows // n) % V7X_BF16_SUBLANES == 0:
            return n, rows // n
    raise ValueError((rows, steps))


def _ada_kernel(c_ref, w_ref, b_ref, o_ref):
    o_ref[...] = _dot(c_ref[...].astype(jnp.bfloat16), w_ref[...].astype(jnp.bfloat16)) + b_ref[...]


def _ada(c, w_ada, b_ada):
    bsz, d = c.shape
    n = w_ada.shape[1]
    return pl.pallas_call(
        _ada_kernel,
        out_shape=jax.ShapeDtypeStruct((bsz, n), jnp.float32),
        grid=(n // ADA_COLS,),
        in_specs=[
            pl.BlockSpec((bsz, d), lambda j: (0, 0)),
            pl.BlockSpec((d, ADA_COLS), lambda j: (0, j)),
            pl.BlockSpec((1, ADA_COLS), lambda j: (0, j)),
        ],
        out_specs=pl.BlockSpec((bsz, ADA_COLS), lambda j: (0, j)),
        compiler_params=pltpu.CompilerParams(
            dimension_semantics=("arbitrary",),
            vmem_limit_bytes=_vmem_limit(0, d * ADA_COLS * 4 + 4 * bsz * d * 4, d * ADA_COLS * 2 + (1 << 20))),
        name="adaln",
    )(c, w_ada, b_ada.reshape(1, n))


def _mixer_kernel(x_ref, mod_ref, gpre_ref, gpost_ref, win_ref, wpool_ref, pscale_ref, convw_ref, convb_ref,
                  wbout_ref, wo_ref, wup_f32, wdown_f32, o_ref, wup_bf16, wdown_bf16, pool_buf, conv_buf,
                  *, tile, d):
    s = pl.program_id(1)
    n_slabs = d // V7X_LANES

    wup_bf16[...] = wup_f32[...].astype(jnp.bfloat16)
    wdown_bf16[...] = wdown_f32[...].astype(jnp.bfloat16)

    @pl.when(s == 0)
    def _():
        _zero_history(pool_buf, POOL_HALO)
        _zero_history(conv_buf, CONV_HALO)

    x = x_ref[0]
    mod = mod_ref[0]
    sh1, sc1, gt1 = mod[0:1], mod[1:2], mod[2:3]
    hb = (_rmsnorm(x, gpre_ref[...]) * (1.0 + sc1) + sh1).astype(jnp.bfloat16)

    def proj(i):
        return _dot(hb, win_ref[:, i * d:(i + 1) * d])

    u = proj(0)
    _put_slabs(pool_buf, POOL_HALO, u, 0)
    gw = d // len(POOL_WINDOWS)
    t1 = (s * tile + lax.broadcasted_iota(jnp.int32, (tile, V7X_LANES), 0) + 1).astype(jnp.float32)
    ya_parts = []
    for g, w in enumerate(POOL_WINDOWS):
        slabs = range(g * gw // V7X_LANES, (g + 1) * gw // V7X_LANES)
        ug = u[:, g * gw:(g + 1) * gw]
        acc = ug
        for k in range(1, w):
            acc = acc + _rows_back(pool_buf, POOL_HALO, tile, slabs, k)
        inv = 1.0 / jnp.minimum(t1, float(w))
        inv = jnp.concatenate([inv] * len(slabs), axis=1)
        ya_parts.append(_dot((acc * inv - ug).astype(jnp.bfloat16), wpool_ref[g]))
    _keep_history(pool_buf, POOL_HALO, tile, range(n_slabs))
    ya = jnp.concatenate(ya_parts, axis=1) * pscale_ref[...]

    v = proj(3) * proj(1)
    _put_slabs(conv_buf, CONV_HALO, v, 0)
    v1 = _rows_back(conv_buf, CONV_HALO, tile, range(n_slabs), 1)
    v2 = _rows_back(conv_buf, CONV_HALO, tile, range(n_slabs), 2)
    _keep_history(conv_buf, CONV_HALO, tile, range(n_slabs))
    cw = convw_ref[...]
    conv = convb_ref[...] + cw[0:1] * v2 + cw[1:2] * v1 + cw[2:3] * v
    yb = _dot((proj(2) * conv).astype(jnp.bfloat16), wbout_ref[...])

    merged = jax.nn.sigmoid(proj(4)) * ya + jax.nn.sigmoid(proj(5)) * yb
    r = _dot(merged.astype(jnp.bfloat16), wo_ref[...])
    o_ref[0] = x + gt1 * _rmsnorm(r, gpost_ref[...])


def _mixer(x, mod, g_pre, g_post, w_in, w_pool, pool_scale, conv_w, conv_b, w_bout, w_o, w_up, w_down):
    bsz, seq, d = x.shape
    tile = MIXER_TILE
    n_seq = seq // tile
    d_in = w_in.shape[1]
    weights = 2 * (w_in.size + w_pool.size + w_bout.size + w_o.size)
    tok = lambda b, s: (b, s, 0)

    def side_spec(w):
        n_blocks, rows = _row_blocks(w.shape[0], bsz * n_seq)
        return pl.BlockSpec((rows, w.shape[1]), lambda b, s: (jnp.minimum(b * n_seq + s, n_blocks - 1), 0)), rows

    (up_spec, up_rows), (down_spec, down_rows) = side_spec(w_up), side_spec(w_down)
    side_bytes = (up_rows * w_up.shape[1] + down_rows * w_down.shape[1]) * (4 + 2)
    return pl.pallas_call(
        functools.partial(_mixer_kernel, tile=tile, d=d),
        out_shape=(jax.ShapeDtypeStruct(x.shape, x.dtype),
                   jax.ShapeDtypeStruct(w_up.shape, jnp.bfloat16), jax.ShapeDtypeStruct(w_down.shape, jnp.bfloat16)),
        grid=(bsz, n_seq),
        in_specs=[
            pl.BlockSpec((1, tile, d), tok),
            pl.BlockSpec((1, N_MOD, d), lambda b, s: (b, 0, 0)),
            _const_spec((1, d)), _const_spec((1, d)),
            _const_spec(w_in.shape), _const_spec(w_pool.shape), _const_spec((1, d)),
            _const_spec(conv_w.shape), _const_spec((1, d)),
            _const_spec(w_bout.shape), _const_spec(w_o.shape),
            up_spec, down_spec,
        ],
        out_specs=(pl.BlockSpec((1, tile, d), tok), up_spec, down_spec),
        scratch_shapes=[pltpu.VMEM((d // V7X_LANES, POOL_HALO + tile, V7X_LANES), jnp.float32),
                        pltpu.VMEM((d // V7X_LANES, CONV_HALO + tile, V7X_LANES), jnp.float32)],
        compiler_params=pltpu.CompilerParams(
            dimension_semantics=("arbitrary", "arbitrary"),
            vmem_limit_bytes=_vmem_limit(weights, 2 * tile * d * 4 + side_bytes, 3 * tile * d_in * 4)),
        name="mixer",
    )(x, mod, g_pre.reshape(1, d), g_post.reshape(1, d), w_in, w_pool, pool_scale.reshape(1, d),
      conv_w, conv_b.reshape(1, d), w_bout, w_o, w_up, w_down)


def _ffn_kernel(x_ref, mod_ref, gpre_ref, gpost_ref, wup_ref, cw_ref, cb_ref, wdown_ref, o_ref, up_buf,
                *, tile, f, chunks):
    s = pl.program_id(1)

    @pl.when(s == 0)
    def _():
        _zero_history(up_buf, CONV_HALO)

    x = x_ref[0]
    mod = mod_ref[0]
    sh2, sc2, gt2 = mod[3:4], mod[4:5], mod[5:6]
    hb = (_rmsnorm(x, gpre_ref[...]) * (1.0 + sc2) + sh2).astype(jnp.bfloat16)

    def conv_cols(lo, n):
        cols = slice(lo, lo + n)
        up = _dot(hb, wup_ref[:, cols])
        slabs = range(lo // V7X_LANES, (lo + n) // V7X_LANES)
        _put_slabs(up_buf, CONV_HALO, up, slabs[0])
        u1 = _rows_back(up_buf, CONV_HALO, tile, slabs, 1)
        u2 = _rows_back(up_buf, CONV_HALO, tile, slabs, 2)
        _keep_history(up_buf, CONV_HALO, tile, slabs)
        cw = cw_ref[:, cols]
        return cb_ref[:, cols] + cw[0:1] * u2 + cw[1:2] * u1 + cw[2:3] * up

    ff = None
    for lo, n in chunks:
        gate = conv_cols(lo, n)
        val = conv_cols(f + lo, n)
        act = (jax.nn.gelu(gate, approximate=True) * val).astype(jnp.bfloat16)
        part = _dot(act, wdown_ref[lo:lo + n, :])
        ff = part if ff is None else ff + part
    o_ref[0] = x + gt2 * _rmsnorm(ff, gpost_ref[...])


def _ffn(x, mod, g_pre, g_post, w_up, conv_w, conv_b, w_down):
    bsz, seq, d = x.shape
    tile = FFN_TILE
    f = w_down.shape[0]
    assert sum(FFN_CHUNKS) == f
    chunks = tuple((sum(FFN_CHUNKS[:i]), n) for i, n in enumerate(FFN_CHUNKS))
    weights = 2 * (w_up.size + w_down.size)
    up_buf_bytes = 2 * f * (CONV_HALO + tile) * 4
    tok = lambda b, s: (b, s, 0)
    return pl.pallas_call(
        functools.partial(_ffn_kernel, tile=tile, f=f, chunks=chunks),
        out_shape=jax.ShapeDtypeStruct(x.shape, x.dtype),
        grid=(bsz, seq // tile),
        in_specs=[
            pl.BlockSpec((1, tile, d), tok),
            pl.BlockSpec((1, N_MOD, d), lambda b, s: (b, 0, 0)),
            _const_spec((1, d)), _const_spec((1, d)),
            _const_spec(w_up.shape), _const_spec(conv_w.shape), _const_spec((1, 2 * f)),
            _const_spec(w_down.shape),
        ],
        out_specs=pl.BlockSpec((1, tile, d), tok),
        scratch_shapes=[pltpu.VMEM((2 * f // V7X_LANES, CONV_HALO + tile, V7X_LANES), jnp.float32)],
        compiler_params=pltpu.CompilerParams(
            dimension_semantics=("arbitrary", "arbitrary"),
            vmem_limit_bytes=_vmem_limit(weights + up_buf_bytes, 2 * tile * d * 4, 3 * tile * 2 * f * 4)),
        name="convffn",
    )(x, mod, g_pre.reshape(1, d), g_post.reshape(1, d), w_up, conv_w, conv_b.reshape(1, 2 * f), w_down)


def kernel(x, c, g_pre_mix, g_post_mix, g_pre_ffn, g_post_ffn, w_ada, b_ada, w_in, w_pool, pool_scale, conv_w,
           conv_b, w_bout, w_o, w_up, ffn_conv_w, ffn_conv_b, w_down):
    bsz, _, d = x.shape
    bf = lambda w: w.astype(jnp.bfloat16)
    for l in range(w_ada.shape[0]):
        mod = _ada(c, w_ada[l], b_ada[l]).reshape(bsz, N_MOD, d)
        x, w_up_bf16, w_down_bf16 = _mixer(x, mod, g_pre_mix[l], g_post_mix[l], bf(w_in[l]), bf(w_pool[l]),
                                           pool_scale[l], conv_w[l], conv_b[l], bf(w_bout[l]), bf(w_o[l]),
                                           w_up[l], w_down[l])
        x = _ffn(x, mod, g_pre_ffn[l], g_post_ffn[l], w_up_bf16, ffn_conv_w[l], ffn_conv_b[l], w_down_bf16)
    return x
```

```python
import functools

import jax
import jax.numpy as jnp
from jax import lax
from jax.experimental import pallas as pl
from jax.experimental.pallas import tpu as pltpu

POOL_WINDOWS = (2, 4, 8, 16)
N_MOD = 6
EPS = 1e-6
GELU_K = 0.7978845608028654
GELU_A = 0.044715

V7X_LANES = 128
V7X_BF16_SUBLANES = 16
V7X_VMEM_BYTES = 64 * 1024 * 1024

POOL_HALO = 16
CONV_HALO = 8

MIXER_TILE = 512
FFN_TILE = 512
FFN_CHUNKS = (1536, 1280)
ADA_COLS = 1536


def _vmem_limit(resident_bytes, streamed_bytes, temp_bytes):
    need = resident_bytes + 2 * streamed_bytes + temp_bytes
    assert need < V7X_VMEM_BYTES, need
    return int(need)


def _const_spec(shape):
    zeros = (0,) * len(shape)
    return pl.BlockSpec(shape, lambda *_: zeros, pipeline_mode=pl.Buffered(1))


def _rmsnorm(xf, g):
    return xf * lax.rsqrt(jnp.mean(xf * xf, axis=-1, keepdims=True) + EPS) * g


def _put_slabs(buf, halo, val, first):
    for i in range(val.shape[1] // V7X_LANES):
        buf[first + i, halo:, :] = val[:, i * V7X_LANES:(i + 1) * V7X_LANES]


def _rows_back(buf, halo, tile, slabs, k):
    return jnp.concatenate([buf[j, halo - k:halo - k + tile, :] for j in slabs], axis=1)


def _keep_history(buf, halo, tile, slabs):
    for j in slabs:
        buf[j, :halo, :] = buf[j, tile:, :]


def _zero_history(buf, halo):
    buf[:, :halo, :] = jnp.zeros((buf.shape[0], halo, V7X_LANES), jnp.float32)


def _dot(a, b):
    return jnp.dot(a, b, preferred_element_type=jnp.float32)


def _row_blocks(rows, steps):
    for n in range(steps, 0, -1):
        if rows % n == 0 and (rows // n) % V7X_BF16_SUBLANES == 0:
            return n, rows // n
    raise ValueError((rows, steps))


def _ada_kernel(c_ref, w_ref, b_ref, o_ref):
    o_ref[...] = _dot(c_ref[...].astype(jnp.bfloat16), w_ref[...].astype(jnp.bfloat16)) + b_ref[...]


def _ada(c, w_ada, b_ada):
    bsz, d = c.shape
    n = w_ada.shape[1]
    return pl.pallas_call(
        _ada_kernel,
        out_shape=jax.ShapeDtypeStruct((bsz, n), jnp.float32),
        grid=(n // ADA_COLS,),
        in_specs=[
            pl.BlockSpec((bsz, d), lambda j: (0, 0)),
            pl.BlockSpec((d, ADA_COLS), lambda j: (0, j)),
            pl.BlockSpec((1, ADA_COLS), lambda j: (0, j)),
        ],
        out_specs=pl.BlockSpec((bsz, ADA_COLS), lambda j: (0, j)),
        compiler_params=pltpu.CompilerParams(
            dimension_semantics=("arbitrary",),
            vmem_limit_bytes=_vmem_limit(0, d * ADA_COLS * 4 + 4 * bsz * d * 4, d * ADA_COLS * 2 + (1 << 20))),
        name="adaln",
    )(c, w_ada, b_ada.reshape(1, n))


def _mixer_kernel(x_ref, mod_ref, gpre_ref, gpost_ref, win_ref, wpool_ref, pscale_ref, convw_ref, convb_ref,
                  wbout_ref, wo_ref, wup_f32, wdown_f32, o_ref, wup_bf16, wdown_bf16, pool_buf, conv_buf,
                  *, tile, d):
    s = pl.program_id(1)
    n_slabs = d // V7X_LANES

    wup_bf16[...] = wup_f32[...].astype(jnp.bfloat16)
    wdown_bf16[...] = wdown_f32[...].astype(jnp.bfloat16)

    @pl.when(s == 0)
    def _():
        _zero_history(pool_buf, POOL_HALO)
        _zero_history(conv_buf, CONV_HALO)

    x = x_ref[0]
    mod = mod_ref[0]
    sh1, sc1, gt1 = mod[0:1], mod[1:2], mod[2:3]
    hb = (_rmsnorm(x, gpre_ref[...]) * (1.0 + sc1) + sh1).astype(jnp.bfloat16)

    def proj(i):
        return _dot(hb, win_ref[:, i * d:(i + 1) * d])

    u = proj(0)
    _put_slabs(pool_buf, POOL_HALO, u, 0)
    v = proj(3) * proj(1)
    _put_slabs(conv_buf, CONV_HALO, v, 0)

    gw = d // len(POOL_WINDOWS)
    t1 = (s * tile + lax.broadcasted_iota(jnp.int32, (tile, V7X_LANES), 0) + 1).astype(jnp.float32)
    pooled = []
    for g, w in enumerate(POOL_WINDOWS):
        slabs = range(g * gw // V7X_LANES, (g + 1) * gw // V7X_LANES)
        ug = u[:, g * gw:(g + 1) * gw]
        acc = ug
        for k in range(1, w):
            acc = acc + _rows_back(pool_buf, POOL_HALO, tile, slabs, k)
        inv = 1.0 / jnp.minimum(t1, float(w))
        inv = jnp.concatenate([inv] * len(slabs), axis=1)
        pooled.append((acc * inv - ug).astype(jnp.bfloat16))
    _keep_history(pool_buf, POOL_HALO, tile, range(n_slabs))

    ub = proj(2)
    ya = jnp.concatenate([_dot(p, wpool_ref[g]) for g, p in enumerate(pooled)], axis=1) * pscale_ref[...]

    v1 = _rows_back(conv_buf, CONV_HALO, tile, range(n_slabs), 1)
    v2 = _rows_back(conv_buf, CONV_HALO, tile, range(n_slabs), 2)
    _keep_history(conv_buf, CONV_HALO, tile, range(n_slabs))
    cw = convw_ref[...]
    conv = convb_ref[...] + cw[0:1] * v2 + cw[1:2] * v1 + cw[2:3] * v
    ubc = (ub * conv).astype(jnp.bfloat16)

    za = proj(4)
    zb = proj(5)
    gated_a = jax.nn.sigmoid(za) * ya
    yb = _dot(ubc, wbout_ref[...])
    merged = gated_a + jax.nn.sigmoid(zb) * yb
    r = _dot(merged.astype(jnp.bfloat16), wo_ref[...])
    o_ref[0] = x + gt1 * _rmsnorm(r, gpost_ref[...])


def _mixer(x, mod, g_pre, g_post, w_in, w_pool, pool_scale, conv_w, conv_b, w_bout, w_o, w_up, w_down):
    bsz, seq, d = x.shape
    tile = MIXER_TILE
    n_seq = seq // tile
    d_in = w_in.shape[1]
    weights = 2 * (w_in.size + w_pool.size + w_bout.size + w_o.size)
    tok = lambda b, s: (b, s, 0)

    def side_spec(w):
        n_blocks, rows = _row_blocks(w.shape[0], bsz * n_seq)
        return pl.BlockSpec((rows, w.shape[1]), lambda b, s: (jnp.minimum(b * n_seq + s, n_blocks - 1), 0)), rows

    (up_spec, up_rows), (down_spec, down_rows) = side_spec(w_up), side_spec(w_down)
    side_bytes = (up_rows * w_up.shape[1] + down_rows * w_down.shape[1]) * (4 + 2)
    hist_bytes = d * (POOL_HALO + CONV_HALO + 2 * tile) * 4
    return pl.pallas_call(
        functools.partial(_mixer_kernel, tile=tile, d=d),
        out_shape=(jax.ShapeDtypeStruct(x.shape, x.dtype),
                   jax.ShapeDtypeStruct(w_up.shape, jnp.bfloat16), jax.ShapeDtypeStruct(w_down.shape, jnp.bfloat16)),
        grid=(bsz, n_seq),
        in_specs=[
            pl.BlockSpec((1, tile, d), tok),
            pl.BlockSpec((1, N_MOD, d), lambda b, s: (b, 0, 0)),
            _const_spec((1, d)), _const_spec((1, d)),
            _const_spec(w_in.shape), _const_spec(w_pool.shape), _const_spec((1, d)),
            _const_spec(conv_w.shape), _const_spec((1, d)),
            _const_spec(w_bout.shape), _const_spec(w_o.shape),
            up_spec, down_spec,
        ],
        out_specs=(pl.BlockSpec((1, tile, d), tok), up_spec, down_spec),
        scratch_shapes=[pltpu.VMEM((d // V7X_LANES, POOL_HALO + tile, V7X_LANES), jnp.float32),
                        pltpu.VMEM((d // V7X_LANES, CONV_HALO + tile, V7X_LANES), jnp.float32)],
        compiler_params=pltpu.CompilerParams(
            dimension_semantics=("arbitrary", "arbitrary"),
            vmem_limit_bytes=_vmem_limit(weights + hist_bytes, 2 * tile * d * 4 + side_bytes, 4 * tile * d * 4)),
        name="mixer",
    )(x, mod, g_pre.reshape(1, d), g_post.reshape(1, d), w_in, w_pool, pool_scale.reshape(1, d),
      conv_w, conv_b.reshape(1, d), w_bout, w_o, w_up, w_down)


def _ffn_kernel(x_ref, mod_ref, gpre_ref, gpost_ref, wup_ref, cw_ref, cb_ref, wdown_ref, o_ref, up_buf,
                *, tile, f, chunks):
    s = pl.program_id(1)

    @pl.when(s == 0)
    def _():
        _zero_history(up_buf, CONV_HALO)

    x = x_ref[0]
    mod = mod_ref[0]
    sh2, sc2, gt2 = mod[3:4], mod[4:5], mod[5:6]
    hb = (_rmsnorm(x, gpre_ref[...]) * (1.0 + sc2) + sh2).astype(jnp.bfloat16)

    def conv_cols(lo, n, scale=None):
        cols = slice(lo, lo + n)
        up = _dot(hb, wup_ref[:, cols])
        slabs = range(lo // V7X_LANES, (lo + n) // V7X_LANES)
        _put_slabs(up_buf, CONV_HALO, up, slabs[0])
        u1 = _rows_back(up_buf, CONV_HALO, tile, slabs, 1)
        u2 = _rows_back(up_buf, CONV_HALO, tile, slabs, 2)
        _keep_history(up_buf, CONV_HALO, tile, slabs)
        cw, cb = cw_ref[:, cols], cb_ref[:, cols]
        if scale is not None:
            cw, cb = cw * scale, cb * scale
        return cb + cw[0:1] * u2 + cw[1:2] * u1 + cw[2:3] * up

    ff = None
    for lo, n in chunks:
        gate = conv_cols(lo, n)
        half_val = conv_cols(f + lo, n, scale=0.5)
        p = gate * half_val
        t = jnp.tanh(gate * (GELU_K + (GELU_K * GELU_A) * (gate * gate)))
        act = (p + p * t).astype(jnp.bfloat16)
        part = _dot(act, wdown_ref[lo:lo + n, :])
        ff = part if ff is None else ff + part
    o_ref[0] = x + gt2 * _rmsnorm(ff, gpost_ref[...])


def _ffn(x, mod, g_pre, g_post, w_up, conv_w, conv_b, w_down):
    bsz, seq, d = x.shape
    tile = FFN_TILE
    f = w_down.shape[0]
    assert sum(FFN_CHUNKS) == f
    chunks = tuple((sum(FFN_CHUNKS[:i]), n) for i, n in enumerate(FFN_CHUNKS))
    weights = 2 * (w_up.size + w_down.size)
    up_buf_bytes = 2 * f * (CONV_HALO + tile) * 4
    tok = lambda b, s: (b, s, 0)
    return pl.pallas_call(
        functools.partial(_ffn_kernel, tile=tile, f=f, chunks=chunks),
        out_shape=jax.ShapeDtypeStruct(x.shape, x.dtype),
        grid=(bsz, seq // tile),
        in_specs=[
            pl.BlockSpec((1, tile, d), tok),
            pl.BlockSpec((1, N_MOD, d), lambda b, s: (b, 0, 0)),
            _const_spec((1, d)), _const_spec((1, d)),
            _const_spec(w_up.shape), _const_spec(conv_w.shape), _const_spec((1, 2 * f)),
            _const_spec(w_down.shape),
        ],
        out_specs=pl.BlockSpec((1, tile, d), tok),
        scratch_shapes=[pltpu.VMEM((2 * f // V7X_LANES, CONV_HALO + tile, V7X_LANES), jnp.float32)],
        compiler_params=pltpu.CompilerParams(
            dimension_semantics=("arbitrary", "arbitrary"),
            vmem_limit_bytes=_vmem_limit(weights + up_buf_bytes, 2 * tile * d * 4, 4 * tile * d * 4)),
        name="convffn",
    )(x, mod, g_pre.reshape(1, d), g_post.reshape(1, d), w_up, conv_w, conv_b.reshape(1, 2 * f), w_down)


def kernel(x, c, g_pre_mix, g_post_mix, g_pre_ffn, g_post_ffn, w_ada, b_ada, w_in, w_pool, pool_scale, conv_w,
           conv_b, w_bout, w_o, w_up, ffn_conv_w, ffn_conv_b, w_down):
    bsz, _, d = x.shape
    bf = lambda w: w.astype(jnp.bfloat16)
    for l in range(w_ada.shape[0]):
        mod = _ada(c, w_ada[l], b_ada[l]).reshape(bsz, N_MOD, d)
        x, w_up_bf16, w_down_bf16 = _mixer(x, mod, g_pre_mix[l], g_post_mix[l], bf(w_in[l]), bf(w_pool[l]),
                                           pool_scale[l], conv_w[l], conv_b[l], bf(w_bout[l]), bf(w_o[l]),
                                           w_up[l], w_down[l])
        x = _ffn(x, mod, g_pre_ffn[l], g_post_ffn[l], w_up_bf16, ffn_conv_w[l], ffn_conv_b[l], w_down_bf16)
    return x
```

```python
import functools

import jax
import jax.numpy as jnp
from jax import lax
from jax.experimental import pallas as pl
from jax.experimental.pallas import tpu as pltpu

POOL_WINDOWS = (2, 4, 8, 16)
N_MOD = 6
EPS = 1e-6
GELU_K = 0.7978845608028654
GELU_A = 0.044715

V7X_LANES = 128
V7X_BF16_SUBLANES = 16
V7X_VMEM_BYTES = 64 * 1024 * 1024

POOL_HALO = 16
CONV_HALO = 8

MIXER_TILE = 512
FFN_TILE = 512
FFN_CHUNKS = (1536, 1280)
ADA_COLS = 1536


def _vmem_limit(resident_bytes, streamed_bytes, temp_bytes):
    need = resident_bytes + 2 * streamed_bytes + temp_bytes
    assert need < V7X_VMEM_BYTES, need
    return int(need)


def _const_spec(shape):
    zeros = (0,) * len(shape)
    return pl.BlockSpec(shape, lambda *_: zeros, pipeline_mode=pl.Buffered(1))


def _rmsnorm(xf, gain_row):
    return xf * lax.rsqrt(jnp.mean(xf * xf, axis=-1, keepdims=True) + EPS) * gain_row


def _put_slabs(buf, halo, val, first):
    for i in range(val.shape[1] // V7X_LANES):
        buf[first + i, halo:, :] = val[:, i * V7X_LANES:(i + 1) * V7X_LANES]


def _rows_back(buf, halo, tile, slabs, k):
    return jnp.concatenate([buf[j, halo - k:halo - k + tile, :] for j in slabs], axis=1)


def _keep_history(buf, halo, tile, slabs):
    for j in slabs:
        buf[j, :halo, :] = buf[j, tile:, :]


def _zero_history(buf, halo):
    buf[:, :halo, :] = jnp.zeros((buf.shape[0], halo, V7X_LANES), jnp.float32)


def _dot(a, b):
    return jnp.dot(a, b, preferred_element_type=jnp.float32)


def _row_blocks(rows, steps):
    for n in range(steps, 0, -1):
        if rows % n == 0 and (rows // n) % V7X_BF16_SUBLANES == 0:
            return n, rows // n
    raise ValueError((rows, steps))


def _ada_kernel(c_ref, w_ref, b_ref, o_ref):
    o_ref[...] = _dot(c_ref[...].astype(jnp.bfloat16), w_ref[...].astype(jnp.bfloat16)) + b_ref[...]


def _ada(c, w_ada, b_ada):
    bsz, d = c.shape
    n = w_ada.shape[1]
    return pl.pallas_call(
        _ada_kernel,
        out_shape=jax.ShapeDtypeStruct((bsz, n), jnp.float32),
        grid=(n // ADA_COLS,),
        in_specs=[
            pl.BlockSpec((bsz, d), lambda j: (0, 0)),
            pl.BlockSpec((d, ADA_COLS), lambda j: (0, j)),
            pl.BlockSpec((1, ADA_COLS), lambda j: (0, j)),
        ],
        out_specs=pl.BlockSpec((bsz, ADA_COLS), lambda j: (0, j)),
        compiler_params=pltpu.CompilerParams(
            dimension_semantics=("arbitrary",),
            vmem_limit_bytes=_vmem_limit(0, d * ADA_COLS * 4 + 4 * bsz * d * 4, d * ADA_COLS * 2 + (1 << 20))),
        name="adaln",
    )(c, w_ada, b_ada.reshape(1, n))


def _mixer_kernel(x_ref, mod_ref, gpre_ref, gpost_ref, win_ref, wpool_ref, pscale_ref, convw_ref, convb_ref,
                  wbout_ref, wo_ref, wup_f32, wdown_f32, o_ref, wup_bf16, wdown_bf16, pool_buf, conv_buf,
                  *, tile, d):
    s = pl.program_id(1)
    n_slabs = d // V7X_LANES

    wup_bf16[...] = wup_f32[...].astype(jnp.bfloat16)
    wdown_bf16[...] = wdown_f32[...].astype(jnp.bfloat16)

    @pl.when(s == 0)
    def _():
        _zero_history(pool_buf, POOL_HALO)
        _zero_history(conv_buf, CONV_HALO)

    x = x_ref[0]
    mod = mod_ref[0]
    sh1, sc1, gt1 = mod[0:1], mod[1:2], mod[2:3]
    hb = (_rmsnorm(x, gpre_ref[...] * (1.0 + sc1)) + sh1).astype(jnp.bfloat16)

    def proj(i):
        return _dot(hb, win_ref[:, i * d:(i + 1) * d])

    u = proj(0)
    _put_slabs(pool_buf, POOL_HALO, u, 0)
    v = proj(3) * proj(1)
    _put_slabs(conv_buf, CONV_HALO, v, 0)

    gw = d // len(POOL_WINDOWS)
    t1 = (s * tile + lax.broadcasted_iota(jnp.int32, (tile, V7X_LANES), 0) + 1).astype(jnp.float32)
    pooled = []
    for g, w in enumerate(POOL_WINDOWS):
        slabs = range(g * gw // V7X_LANES, (g + 1) * gw // V7X_LANES)
        ug = u[:, g * gw:(g + 1) * gw]
        acc = ug
        for k in range(1, w):
            acc = acc + _rows_back(pool_buf, POOL_HALO, tile, slabs, k)
        inv = 1.0 / jnp.minimum(t1, float(w))
        inv = jnp.concatenate([inv] * len(slabs), axis=1)
        pooled.append((acc * inv - ug).astype(jnp.bfloat16))
    _keep_history(pool_buf, POOL_HALO, tile, range(n_slabs))

    ub = proj(2)
    ya = jnp.concatenate([_dot(p, wpool_ref[g]) for g, p in enumerate(pooled)], axis=1) * pscale_ref[...]

    v1 = _rows_back(conv_buf, CONV_HALO, tile, range(n_slabs), 1)
    v2 = _rows_back(conv_buf, CONV_HALO, tile, range(n_slabs), 2)
    _keep_history(conv_buf, CONV_HALO, tile, range(n_slabs))
    cw = convw_ref[...]
    conv = convb_ref[...] + cw[0:1] * v2 + cw[1:2] * v1 + cw[2:3] * v
    ubc = (ub * conv).astype(jnp.bfloat16)

    za = proj(4)
    zb = proj(5)
    gated_a = jax.nn.sigmoid(za) * ya
    yb = _dot(ubc, wbout_ref[...])
    merged = gated_a + jax.nn.sigmoid(zb) * yb
    r = _dot(merged.astype(jnp.bfloat16), wo_ref[...])
    o_ref[0] = x + _rmsnorm(r, gpost_ref[...] * gt1)


def _mixer(x, mod, g_pre, g_post, w_in, w_pool, pool_scale, conv_w, conv_b, w_bout, w_o, w_up, w_down):
    bsz, seq, d = x.shape
    tile = MIXER_TILE
    n_seq = seq // tile
    d_in = w_in.shape[1]
    weights = 2 * (w_in.size + w_pool.size + w_bout.size + w_o.size)
    tok = lambda b, s: (b, s, 0)

    def side_spec(w):
        n_blocks, rows = _row_blocks(w.shape[0], bsz * n_seq)
        return pl.BlockSpec((rows, w.shape[1]), lambda b, s: (jnp.minimum(b * n_seq + s, n_blocks - 1), 0)), rows

    (up_spec, up_rows), (down_spec, down_rows) = side_spec(w_up), side_spec(w_down)
    side_bytes = (up_rows * w_up.shape[1] + down_rows * w_down.shape[1]) * (4 + 2)
    hist_bytes = d * (POOL_HALO + CONV_HALO + 2 * tile) * 4
    return pl.pallas_call(
        functools.partial(_mixer_kernel, tile=tile, d=d),
        out_shape=(jax.ShapeDtypeStruct(x.shape, x.dtype),
                   jax.ShapeDtypeStruct(w_up.shape, jnp.bfloat16), jax.ShapeDtypeStruct(w_down.shape, jnp.bfloat16)),
        grid=(bsz, n_seq),
        in_specs=[
            pl.BlockSpec((1, tile, d), tok),
            pl.BlockSpec((1, N_MOD, d), lambda b, s: (b, 0, 0)),
            _const_spec((1, d)), _const_spec((1, d)),
            _const_spec(w_in.shape), _const_spec(w_pool.shape), _const_spec((1, d)),
            _const_spec(conv_w.shape), _const_spec((1, d)),
            _const_spec(w_bout.shape), _const_spec(w_o.shape),
            up_spec, down_spec,
        ],
        out_specs=(pl.BlockSpec((1, tile, d), tok), up_spec, down_spec),
        scratch_shapes=[pltpu.VMEM((d // V7X_LANES, POOL_HALO + tile, V7X_LANES), jnp.float32),
                        pltpu.VMEM((d // V7X_LANES, CONV_HALO + tile, V7X_LANES), jnp.float32)],
        compiler_params=pltpu.CompilerParams(
            dimension_semantics=("arbitrary", "arbitrary"),
            vmem_limit_bytes=_vmem_limit(weights + hist_bytes, 2 * tile * d * 4 + side_bytes, 4 * tile * d * 4)),
        name="mixer",
    )(x, mod, g_pre.reshape(1, d), g_post.reshape(1, d), w_in, w_pool, pool_scale.reshape(1, d),
      conv_w, conv_b.reshape(1, d), w_bout, w_o, w_up, w_down)


def _ffn_kernel(x_ref, mod_ref, gpre_ref, gpost_ref, wup_ref, cw_ref, cb_ref, wdown_ref, o_ref, up_buf,
                *, tile, f, chunks):
    s = pl.program_id(1)

    @pl.when(s == 0)
    def _():
        _zero_history(up_buf, CONV_HALO)

    x = x_ref[0]
    mod = mod_ref[0]
    sh2, sc2, gt2 = mod[3:4], mod[4:5], mod[5:6]
    hb = (_rmsnorm(x, gpre_ref[...] * (1.0 + sc2)) + sh2).astype(jnp.bfloat16)

    def conv_cols(lo, n, scale=None):
        cols = slice(lo, lo + n)
        up = _dot(hb, wup_ref[:, cols])
        slabs = range(lo // V7X_LANES, (lo + n) // V7X_LANES)
        _put_slabs(up_buf, CONV_HALO, up, slabs[0])
        u1 = _rows_back(up_buf, CONV_HALO, tile, slabs, 1)
        u2 = _rows_back(up_buf, CONV_HALO, tile, slabs, 2)
        _keep_history(up_buf, CONV_HALO, tile, slabs)
        cw, cb = cw_ref[:, cols], cb_ref[:, cols]
        if scale is not None:
            cw, cb = cw * scale, cb * scale
        return cb + cw[0:1] * u2 + cw[1:2] * u1 + cw[2:3] * up

    ff = None
    for lo, n in chunks:
        gate = conv_cols(lo, n)
        half_val = conv_cols(f + lo, n, scale=0.5)
        p = gate * half_val
        t = jnp.tanh(gate * (GELU_K + (GELU_K * GELU_A) * (gate * gate)))
        act = (p + p * t).astype(jnp.bfloat16)
        part = _dot(act, wdown_ref[lo:lo + n, :])
        ff = part if ff is None else ff + part
    o_ref[0] = x + _rmsnorm(ff, gpost_ref[...] * gt2)


def _ffn(x, mod, g_pre, g_post, w_up, conv_w, conv_b, w_down):
    bsz, seq, d = x.shape
    tile = FFN_TILE
    f = w_down.shape[0]
    assert sum(FFN_CHUNKS) == f
    chunks = tuple((sum(FFN_CHUNKS[:i]), n) for i, n in enumerate(FFN_CHUNKS))
    weights = 2 * (w_up.size + w_down.size)
    up_buf_bytes = 2 * f * (CONV_HALO + tile) * 4
    tok = lambda b, s: (b, s, 0)
    return pl.pallas_call(
        functools.partial(_ffn_kernel, tile=tile, f=f, chunks=chunks),
        out_shape=jax.ShapeDtypeStruct(x.shape, x.dtype),
        grid=(bsz, seq // tile),
        in_specs=[
            pl.BlockSpec((1, tile, d), tok),
            pl.BlockSpec((1, N_MOD, d), lambda b, s: (b, 0, 0)),
            _const_spec((1, d)), _const_spec((1, d)),
            _const_spec(w_up.shape), _const_spec(conv_w.shape), _const_spec((1, 2 * f)),
            _const_spec(w_down.shape),
        ],
        out_specs=pl.BlockSpec((1, tile, d), tok),
        scratch_shapes=[pltpu.VMEM((2 * f // V7X_LANES, CONV_HALO + tile, V7X_LANES), jnp.float32)],
        compiler_params=pltpu.CompilerParams(
            dimension_semantics=("arbitrary", "arbitrary"),
            vmem_limit_bytes=_vmem_limit(weights + up_buf_bytes, 2 * tile * d * 4, 4 * tile * d * 4)),
        name="convffn",
    )(x, mod, g_pre.reshape(1, d), g_post.reshape(1, d), w_up, conv_w, conv_b.reshape(1, 2 * f), w_down)


def kernel(x, c, g_pre_mix, g_post_mix, g_pre_ffn, g_post_ffn, w_ada, b_ada, w_in, w_pool, pool_scale, conv_w,
           conv_b, w_bout, w_o, w_up, ffn_conv_w, ffn_conv_b, w_down):
    bsz, _, d = x.shape
    bf = lambda w: w.astype(jnp.bfloat16)
    for l in range(w_ada.shape[0]):
        mod = _ada(c, w_ada[l], b_ada[l]).reshape(bsz, N_MOD, d)
        x, w_up_bf16, w_down_bf16 = _mixer(x, mod, g_pre_mix[l], g_post_mix[l], bf(w_in[l]), bf(w_pool[l]),
                                           pool_scale[l], conv_w[l], conv_b[l], bf(w_bout[l]), bf(w_o[l]),
                                           w_up[l], w_down[l])
        x = _ffn(x, mod, g_pre_ffn[l], g_post_ffn[l], w_up_bf16, ffn_conv_w[l], ffn_conv_b[l], w_down_bf16)
    return x
```

```python
import functools

import jax
import jax.numpy as jnp
from jax import lax
from jax.experimental import pallas as pl
from jax.experimental.pallas import tpu as pltpu

POOL_WINDOWS = (2, 4, 8, 16)
N_MOD = 6
EPS = 1e-6
GELU_K = 0.7978845608028654
GELU_A = 0.044715

V7X_LANES = 128
V7X_BF16_SUBLANES = 16
V7X_VMEM_BYTES = 64 * 1024 * 1024

POOL_HALO = 16
CONV_HALO = 8

MIXER_TILE = 512
FFN_TILE = 512
ROW_SPLIT = 2
FFN_CHUNKS = (1536, 1280)
ADA_COLS = 1536


def _vmem_limit(resident_bytes, streamed_bytes, temp_bytes):
    need = resident_bytes + 2 * streamed_bytes + temp_bytes
    assert need < V7X_VMEM_BYTES, need
    return int(need)


def _const_spec(shape):
    zeros = (0,) * len(shape)
    return pl.BlockSpec(shape, lambda *_: zeros, pipeline_mode=pl.Buffered(1))


def _rmsnorm(xf, gain_row):
    return xf * lax.rsqrt(jnp.mean(xf * xf, axis=-1, keepdims=True) + EPS) * gain_row


def _put_slabs(buf, halo, val, first):
    for i in range(val.shape[1] // V7X_LANES):
        buf[first + i, halo:, :] = val[:, i * V7X_LANES:(i + 1) * V7X_LANES]


def _rows_back(buf, halo, tile, slabs, k):
    return jnp.concatenate([buf[j, halo - k:halo - k + tile, :] for j in slabs], axis=1)


def _keep_history(buf, halo, tile, slabs):
    for j in slabs:
        buf[j, :halo, :] = buf[j, tile:, :]


def _zero_history(buf, halo):
    buf[:, :halo, :] = jnp.zeros((buf.shape[0], halo, V7X_LANES), jnp.float32)


def _dot(a, b):
    return jnp.dot(a, b, preferred_element_type=jnp.float32)


def _row_blocks(rows, steps):
    for n in range(steps, 0, -1):
        if rows % n == 0 and (rows // n) % V7X_BF16_SUBLANES == 0:
            return n, rows // n
    raise ValueError((rows, steps))


def _ada_kernel(c_ref, w_ref, b_ref, o_ref):
    o_ref[...] = _dot(c_ref[...].astype(jnp.bfloat16), w_ref[...].astype(jnp.bfloat16)) + b_ref[...]


def _ada(c, w_ada, b_ada):
    bsz, d = c.shape
    n = w_ada.shape[1]
    return pl.pallas_call(
        _ada_kernel,
        out_shape=jax.ShapeDtypeStruct((bsz, n), jnp.float32),
        grid=(n // ADA_COLS,),
        in_specs=[
            pl.BlockSpec((bsz, d), lambda j: (0, 0)),
            pl.BlockSpec((d, ADA_COLS), lambda j: (0, j)),
            pl.BlockSpec((1, ADA_COLS), lambda j: (0, j)),
        ],
        out_specs=pl.BlockSpec((bsz, ADA_COLS), lambda j: (0, j)),
        compiler_params=pltpu.CompilerParams(
            dimension_semantics=("arbitrary",),
            vmem_limit_bytes=_vmem_limit(0, d * ADA_COLS * 4 + 4 * bsz * d * 4, d * ADA_COLS * 2 + (1 << 20))),
        name="adaln",
    )(c, w_ada, b_ada.reshape(1, n))


def _mixer_kernel(x_ref, mod_ref, gpre_ref, gpost_ref, win_ref, wpool_ref, pscale_ref, convw_ref, convb_ref,
                  wbout_ref, wo_ref, wup_f32, wdown_f32, o_ref, wup_bf16, wdown_bf16, pool_buf, conv_buf,
                  *, tile, d):
    s = pl.program_id(1)
    n_slabs = d // V7X_LANES

    wup_bf16[...] = wup_f32[...].astype(jnp.bfloat16)
    wdown_bf16[...] = wdown_f32[...].astype(jnp.bfloat16)

    @pl.when(s == 0)
    def _():
        _zero_history(pool_buf, POOL_HALO)
        _zero_history(conv_buf, CONV_HALO)

    x = x_ref[0]
    mod = mod_ref[0]
    sh1, sc1, gt1 = mod[0:1], mod[1:2], mod[2:3]
    halves = [slice(i * tile // ROW_SPLIT, (i + 1) * tile // ROW_SPLIT) for i in range(ROW_SPLIT)]
    pre_gain = gpre_ref[...] * (1.0 + sc1)
    hb_parts = [(_rmsnorm(x[rows], pre_gain) + sh1).astype(jnp.bfloat16) for rows in halves]
    hb = jnp.concatenate(hb_parts, axis=0)

    def proj(i):
        return _dot(hb, win_ref[:, i * d:(i + 1) * d])

    u = jnp.concatenate([_dot(part, win_ref[:, 0:d]) for part in hb_parts], axis=0)
    _put_slabs(pool_buf, POOL_HALO, u, 0)
    v = proj(3) * proj(1)
    _put_slabs(conv_buf, CONV_HALO, v, 0)

    gw = d // len(POOL_WINDOWS)
    t1 = (s * tile + lax.broadcasted_iota(jnp.int32, (tile, V7X_LANES), 0) + 1).astype(jnp.float32)
    pooled = []
    for g, w in enumerate(POOL_WINDOWS):
        slabs = range(g * gw // V7X_LANES, (g + 1) * gw // V7X_LANES)
        ug = u[:, g * gw:(g + 1) * gw]
        acc = ug
        for k in range(1, w):
            acc = acc + _rows_back(pool_buf, POOL_HALO, tile, slabs, k)
        inv = 1.0 / jnp.minimum(t1, float(w))
        inv = jnp.concatenate([inv] * len(slabs), axis=1)
        pooled.append((acc * inv - ug).astype(jnp.bfloat16))
    _keep_history(pool_buf, POOL_HALO, tile, range(n_slabs))

    ub = proj(2)
    half_ya = jnp.concatenate([_dot(p, wpool_ref[g]) for g, p in enumerate(pooled)], axis=1) * (0.5 * pscale_ref[...])

    v1 = _rows_back(conv_buf, CONV_HALO, tile, range(n_slabs), 1)
    v2 = _rows_back(conv_buf, CONV_HALO, tile, range(n_slabs), 2)
    _keep_history(conv_buf, CONV_HALO, tile, range(n_slabs))
    cw, cb = 0.5 * convw_ref[...], 0.5 * convb_ref[...]
    half_conv = cb + cw[0:1] * v2 + cw[1:2] * v1 + cw[2:3] * v
    half_ubc = (ub * half_conv).astype(jnp.bfloat16)

    za = proj(4)
    zb = proj(5)
    gated_a = half_ya + half_ya * jnp.tanh(0.5 * za)
    half_yb = _dot(half_ubc, wbout_ref[...])
    merged = (gated_a + (half_yb + half_yb * jnp.tanh(0.5 * zb))).astype(jnp.bfloat16)
    post_gain = gpost_ref[...] * gt1
    for rows in halves:
        o_ref[0, rows, :] = x[rows] + _rmsnorm(_dot(merged[rows], wo_ref[...]), post_gain)


def _mixer(x, mod, g_pre, g_post, w_in, w_pool, pool_scale, conv_w, conv_b, w_bout, w_o, w_up, w_down):
    bsz, seq, d = x.shape
    tile = MIXER_TILE
    n_seq = seq // tile
    d_in = w_in.shape[1]
    weights = 2 * (w_in.size + w_pool.size + w_bout.size + w_o.size)
    tok = lambda b, s: (b, s, 0)

    def side_spec(w):
        n_blocks, rows = _row_blocks(w.shape[0], bsz * n_seq)
        return pl.BlockSpec((rows, w.shape[1]), lambda b, s: (jnp.minimum(b * n_seq + s, n_blocks - 1), 0)), rows

    (up_spec, up_rows), (down_spec, down_rows) = side_spec(w_up), side_spec(w_down)
    side_bytes = (up_rows * w_up.shape[1] + down_rows * w_down.shape[1]) * (4 + 2)
    hist_bytes = d * (POOL_HALO + CONV_HALO + 2 * tile) * 4
    return pl.pallas_call(
        functools.partial(_mixer_kernel, tile=tile, d=d),
        out_shape=(jax.ShapeDtypeStruct(x.shape, x.dtype),
                   jax.ShapeDtypeStruct(w_up.shape, jnp.bfloat16), jax.ShapeDtypeStruct(w_down.shape, jnp.bfloat16)),
        grid=(bsz, n_seq),
        in_specs=[
            pl.BlockSpec((1, tile, d), tok),
            pl.BlockSpec((1, N_MOD, d), lambda b, s: (b, 0, 0)),
            _const_spec((1, d)), _const_spec((1, d)),
            _const_spec(w_in.shape), _const_spec(w_pool.shape), _const_spec((1, d)),
            _const_spec(conv_w.shape), _const_spec((1, d)),
            _const_spec(w_bout.shape), _const_spec(w_o.shape),
            up_spec, down_spec,
        ],
        out_specs=(pl.BlockSpec((1, tile, d), tok), up_spec, down_spec),
        scratch_shapes=[pltpu.VMEM((d // V7X_LANES, POOL_HALO + tile, V7X_LANES), jnp.float32),
                        pltpu.VMEM((d // V7X_LANES, CONV_HALO + tile, V7X_LANES), jnp.float32)],
        compiler_params=pltpu.CompilerParams(
            dimension_semantics=("arbitrary", "arbitrary"),
            vmem_limit_bytes=_vmem_limit(weights + hist_bytes, 2 * tile * d * 4 + side_bytes, 4 * tile * d * 4)),
        name="mixer",
    )(x, mod, g_pre.reshape(1, d), g_post.reshape(1, d), w_in, w_pool, pool_scale.reshape(1, d),
      conv_w, conv_b.reshape(1, d), w_bout, w_o, w_up, w_down)


def _ffn_kernel(x_ref, mod_ref, gpre_ref, gpost_ref, wup_ref, cw_ref, cb_ref, wdown_ref, o_ref, up_buf,
                *, tile, f, chunks):
    s = pl.program_id(1)

    @pl.when(s == 0)
    def _():
        _zero_history(up_buf, CONV_HALO)

    x = x_ref[0]
    mod = mod_ref[0]
    sh2, sc2, gt2 = mod[3:4], mod[4:5], mod[5:6]
    hb = (_rmsnorm(x, gpre_ref[...] * (1.0 + sc2)) + sh2).astype(jnp.bfloat16)

    def conv_cols(lo, n, scale=None):
        cols = slice(lo, lo + n)
        up = _dot(hb, wup_ref[:, cols])
        slabs = range(lo // V7X_LANES, (lo + n) // V7X_LANES)
        _put_slabs(up_buf, CONV_HALO, up, slabs[0])
        u1 = _rows_back(up_buf, CONV_HALO, tile, slabs, 1)
        u2 = _rows_back(up_buf, CONV_HALO, tile, slabs, 2)
        _keep_history(up_buf, CONV_HALO, tile, slabs)
        cw, cb = cw_ref[:, cols], cb_ref[:, cols]
        if scale is not None:
            cw, cb = cw * scale, cb * scale
        return cb + cw[0:1] * u2 + cw[1:2] * u1 + cw[2:3] * up

    ff = None
    for lo, n in chunks:
        gate = conv_cols(lo, n)
        half_val = conv_cols(f + lo, n, scale=0.5)
        p = gate * half_val
        t = jnp.tanh(gate * (GELU_K + (GELU_K * GELU_A) * (gate * gate)))
        act = (p + p * t).astype(jnp.bfloat16)
        part = _dot(act, wdown_ref[lo:lo + n, :])
        ff = part if ff is None else ff + part
    o_ref[0] = x + _rmsnorm(ff, gpost_ref[...] * gt2)


def _ffn(x, mod, g_pre, g_post, w_up, conv_w, conv_b, w_down):
    bsz, seq, d = x.shape
    tile = FFN_TILE
    f = w_down.shape[0]
    assert sum(FFN_CHUNKS) == f
    chunks = tuple((sum(FFN_CHUNKS[:i]), n) for i, n in enumerate(FFN_CHUNKS))
    weights = 2 * (w_up.size + w_down.size)
    up_buf_bytes = 2 * f * (CONV_HALO + tile) * 4
    tok = lambda b, s: (b, s, 0)
    return pl.pallas_call(
        functools.partial(_ffn_kernel, tile=tile, f=f, chunks=chunks),
        out_shape=jax.ShapeDtypeStruct(x.shape, x.dtype),
        grid=(bsz, seq // tile),
        in_specs=[
            pl.BlockSpec((1, tile, d), tok),
            pl.BlockSpec((1, N_MOD, d), lambda b, s: (b, 0, 0)),
            _const_spec((1, d)), _const_spec((1, d)),
            _const_spec(w_up.shape), _const_spec(conv_w.shape), _const_spec((1, 2 * f)),
            _const_spec(w_down.shape),
        ],
        out_specs=pl.BlockSpec((1, tile, d), tok),
        scratch_shapes=[pltpu.VMEM((2 * f // V7X_LANES, CONV_HALO + tile, V7X_LANES), jnp.float32)],
        compiler_params=pltpu.CompilerParams(
            dimension_semantics=("arbitrary", "arbitrary"),
            vmem_limit_bytes=_vmem_limit(weights + up_buf_bytes, 2 * tile * d * 4, 4 * tile * d * 4)),
        name="convffn",
    )(x, mod, g_pre.reshape(1, d), g_post.reshape(1, d), w_up, conv_w, conv_b.reshape(1, 2 * f), w_down)


def kernel(x, c, g_pre_mix, g_post_mix, g_pre_ffn, g_post_ffn, w_ada, b_ada, w_in, w_pool, pool_scale, conv_w,
           conv_b, w_bout, w_o, w_up, ffn_conv_w, ffn_conv_b, w_down):
    bsz, _, d = x.shape
    bf = lambda w: w.astype(jnp.bfloat16)
    for l in range(w_ada.shape[0]):
        mod = _ada(c, w_ada[l], b_ada[l]).reshape(bsz, N_MOD, d)
        x, w_up_bf16, w_down_bf16 = _mixer(x, mod, g_pre_mix[l], g_post_mix[l], bf(w_in[l]), bf(w_pool[l]),
                                           pool_scale[l], conv_w[l], conv_b[l], bf(w_bout[l]), bf(w_o[l]),
                                           w_up[l], w_down[l])
        x = _ffn(x, mod, g_pre_ffn[l], g_post_ffn[l], w_up_bf16, ffn_conv_w[l], ffn_conv_b[l], w_down_bf16)
    return x
```

```python
import functools

import jax
import jax.numpy as jnp
from jax import lax
from jax.experimental import pallas as pl
from jax.experimental.pallas import tpu as pltpu

POOL_WINDOWS = (2, 4, 8, 16)
N_MOD = 6
EPS = 1e-6
GELU_K = 0.7978845608028654
GELU_A = 0.044715

V7X_LANES = 128
V7X_BF16_SUBLANES = 16
V7X_VMEM_BYTES = 64 * 1024 * 1024

POOL_HALO = 16
CONV_HALO = 8

MIXER_TILE = 512
FFN_TILE = 512
ROW_SPLIT = 2
FFN_CHUNKS = (1536, 1280)


def _vmem_limit(resident_bytes, streamed_bytes, temp_bytes):
    need = resident_bytes + 2 * streamed_bytes + temp_bytes
    assert need < V7X_VMEM_BYTES, need
    return int(need)


def _const_spec(shape):
    zeros = (0,) * len(shape)
    return pl.BlockSpec(shape, lambda *_: zeros, pipeline_mode=pl.Buffered(1))


def _rmsnorm(xf, gain_row):
    return xf * lax.rsqrt(jnp.mean(xf * xf, axis=-1, keepdims=True) + EPS) * gain_row


def _put_slabs(buf, halo, val, first):
    for i in range(val.shape[1] // V7X_LANES):
        buf[first + i, halo:, :] = val[:, i * V7X_LANES:(i + 1) * V7X_LANES]


def _rows_back(buf, halo, tile, slabs, k):
    return jnp.concatenate([buf[j, halo - k:halo - k + tile, :] for j in slabs], axis=1)


def _keep_history(buf, halo, tile, slabs):
    for j in slabs:
        buf[j, :halo, :] = buf[j, tile:, :]


def _zero_history(buf, halo):
    buf[:, :halo, :] = jnp.zeros((buf.shape[0], halo, V7X_LANES), jnp.float32)


def _dot(a, b):
    return jnp.dot(a, b, preferred_element_type=jnp.float32)


def _row_blocks(rows, steps):
    for n in range(steps, 0, -1):
        if rows % n == 0 and (rows // n) % V7X_BF16_SUBLANES == 0:
            return n, rows // n
    raise ValueError((rows, steps))


def _ada_kernel(c_ref, w_ref, b_ref, win_ref, wpool_ref, pscale_ref, o_ref, wfold_ref):
    o_ref[...] = _dot(c_ref[...].astype(jnp.bfloat16), w_ref[...].astype(jnp.bfloat16)) + b_ref[...]
    wp = wpool_ref[0] * (0.5 * pscale_ref[...])
    wfold = jnp.dot(win_ref[...], wp, precision=lax.Precision.HIGHEST, preferred_element_type=jnp.float32)
    wfold_ref[...] = wfold.astype(jnp.bfloat16)


def _ada(c, w_ada, b_ada, w_in, w_pool, pool_scale):
    bsz, d = c.shape
    n = w_ada.shape[1]
    groups, gw, _ = w_pool.shape
    assert n % groups == 0 and groups * gw == d
    ada_cols = n // groups
    return pl.pallas_call(
        _ada_kernel,
        out_shape=(jax.ShapeDtypeStruct((bsz, n), jnp.float32), jax.ShapeDtypeStruct((d, d), jnp.bfloat16)),
        grid=(groups,),
        in_specs=[
            pl.BlockSpec((bsz, d), lambda j: (0, 0)),
            pl.BlockSpec((d, ada_cols), lambda j: (0, j)),
            pl.BlockSpec((1, ada_cols), lambda j: (0, j)),
            pl.BlockSpec((d, gw), lambda j: (0, j)),
            pl.BlockSpec((1, gw, gw), lambda j: (j, 0, 0)),
            pl.BlockSpec((1, gw), lambda j: (0, j)),
        ],
        out_specs=(pl.BlockSpec((bsz, ada_cols), lambda j: (0, j)), pl.BlockSpec((d, gw), lambda j: (0, j))),
        compiler_params=pltpu.CompilerParams(
            dimension_semantics=("arbitrary",),
            vmem_limit_bytes=_vmem_limit(0, d * ada_cols * 4 + d * gw * 6 + gw * gw * 4 + 4 * bsz * d * 4,
                                         d * ada_cols * 2 + 8 * d * gw * 4)),
        name="adaln",
    )(c, w_ada, b_ada.reshape(1, n), w_in, w_pool, pool_scale.reshape(1, d))


def _mixer_kernel(x_ref, mod_ref, gpre_ref, gpost_ref, wfold_ref, win_ref, convw_ref, convb_ref,
                  wbout_ref, wo_ref, wup_f32, wdown_f32, o_ref, wup_bf16, wdown_bf16, pool_buf, conv_buf,
                  *, tile, d):
    s = pl.program_id(1)
    n_slabs = d // V7X_LANES

    wup_bf16[...] = wup_f32[...].astype(jnp.bfloat16)
    wdown_bf16[...] = wdown_f32[...].astype(jnp.bfloat16)

    @pl.when(s == 0)
    def _():
        _zero_history(pool_buf, POOL_HALO)
        _zero_history(conv_buf, CONV_HALO)

    x = x_ref[0]
    mod = mod_ref[0]
    sh1, sc1, gt1 = mod[0:1], mod[1:2], mod[2:3]
    halves = [slice(i * tile // ROW_SPLIT, (i + 1) * tile // ROW_SPLIT) for i in range(ROW_SPLIT)]
    pre_gain = gpre_ref[...] * (1.0 + sc1)
    hb_parts = [(_rmsnorm(x[rows], pre_gain) + sh1).astype(jnp.bfloat16) for rows in halves]
    hb = jnp.concatenate(hb_parts, axis=0)

    def proj(i):
        return _dot(hb, win_ref[:, (i - 1) * d:i * d])

    u = jnp.concatenate([_dot(part, wfold_ref[...]) for part in hb_parts], axis=0)
    _put_slabs(pool_buf, POOL_HALO, u, 0)
    v = proj(3) * proj(1)
    _put_slabs(conv_buf, CONV_HALO, v, 0)

    gw = d // len(POOL_WINDOWS)
    t1 = (s * tile + lax.broadcasted_iota(jnp.int32, (tile, V7X_LANES), 0) + 1).astype(jnp.float32)
    half_ya = []
    for g, w in enumerate(POOL_WINDOWS):
        slabs = range(g * gw // V7X_LANES, (g + 1) * gw // V7X_LANES)
        ug = u[:, g * gw:(g + 1) * gw]
        acc = ug
        for k in range(1, w):
            acc = acc + _rows_back(pool_buf, POOL_HALO, tile, slabs, k)
        inv = 1.0 / jnp.minimum(t1, float(w))
        inv = jnp.concatenate([inv] * len(slabs), axis=1)
        half_ya.append(acc * inv - ug)
    _keep_history(pool_buf, POOL_HALO, tile, range(n_slabs))
    half_ya = jnp.concatenate(half_ya, axis=1)

    ub = proj(2)

    v1 = _rows_back(conv_buf, CONV_HALO, tile, range(n_slabs), 1)
    v2 = _rows_back(conv_buf, CONV_HALO, tile, range(n_slabs), 2)
    _keep_history(conv_buf, CONV_HALO, tile, range(n_slabs))
    cw, cb = 0.5 * convw_ref[...], 0.5 * convb_ref[...]
    half_conv = cb + cw[0:1] * v2 + cw[1:2] * v1 + cw[2:3] * v
    half_ubc = (ub * half_conv).astype(jnp.bfloat16)

    za = proj(4)
    zb = proj(5)
    gated_a = half_ya + half_ya * jnp.tanh(0.5 * za)
    half_yb = _dot(half_ubc, wbout_ref[...])
    merged = (gated_a + (half_yb + half_yb * jnp.tanh(0.5 * zb))).astype(jnp.bfloat16)
    post_gain = gpost_ref[...] * gt1
    for rows in halves:
        o_ref[0, rows, :] = x[rows] + _rmsnorm(_dot(merged[rows], wo_ref[...]), post_gain)


def _mixer(x, mod, g_pre, g_post, w_fold, w_in_rest, conv_w, conv_b, w_bout, w_o, w_up, w_down):
    bsz, seq, d = x.shape
    tile = MIXER_TILE
    n_seq = seq // tile
    weights = 2 * (w_fold.size + w_in_rest.size + w_bout.size + w_o.size)
    tok = lambda b, s: (b, s, 0)

    def side_spec(w):
        n_blocks, rows = _row_blocks(w.shape[0], bsz * n_seq)
        return pl.BlockSpec((rows, w.shape[1]), lambda b, s: (jnp.minimum(b * n_seq + s, n_blocks - 1), 0)), rows

    (up_spec, up_rows), (down_spec, down_rows) = side_spec(w_up), side_spec(w_down)
    side_bytes = (up_rows * w_up.shape[1] + down_rows * w_down.shape[1]) * (4 + 2)
    hist_bytes = d * (POOL_HALO + CONV_HALO + 2 * tile) * 4
    return pl.pallas_call(
        functools.partial(_mixer_kernel, tile=tile, d=d),
        out_shape=(jax.ShapeDtypeStruct(x.shape, x.dtype),
                   jax.ShapeDtypeStruct(w_up.shape, jnp.bfloat16), jax.ShapeDtypeStruct(w_down.shape, jnp.bfloat16)),
        grid=(bsz, n_seq),
        in_specs=[
            pl.BlockSpec((1, tile, d), tok),
            pl.BlockSpec((1, N_MOD, d), lambda b, s: (b, 0, 0)),
            _const_spec((1, d)), _const_spec((1, d)),
            _const_spec(w_fold.shape), _const_spec(w_in_rest.shape),
            _const_spec(conv_w.shape), _const_spec((1, d)),
            _const_spec(w_bout.shape), _const_spec(w_o.shape),
            up_spec, down_spec,
        ],
        out_specs=(pl.BlockSpec((1, tile, d), tok), up_spec, down_spec),
        scratch_shapes=[pltpu.VMEM((d // V7X_LANES, POOL_HALO + tile, V7X_LANES), jnp.float32),
                        pltpu.VMEM((d // V7X_LANES, CONV_HALO + tile, V7X_LANES), jnp.float32)],
        compiler_params=pltpu.CompilerParams(
            dimension_semantics=("arbitrary", "arbitrary"),
            vmem_limit_bytes=_vmem_limit(weights + hist_bytes, 2 * tile * d * 4 + side_bytes, 4 * tile * d * 4)),
        name="mixer",
    )(x, mod, g_pre.reshape(1, d), g_post.reshape(1, d), w_fold, w_in_rest,
      conv_w, conv_b.reshape(1, d), w_bout, w_o, w_up, w_down)


def _ffn_kernel(x_ref, mod_ref, gpre_ref, gpost_ref, wup_ref, cw_ref, cb_ref, wdown_ref, o_ref, up_buf,
                *, tile, f, chunks):
    s = pl.program_id(1)

    @pl.when(s == 0)
    def _():
        _zero_history(up_buf, CONV_HALO)

    x = x_ref[0]
    mod = mod_ref[0]
    sh2, sc2, gt2 = mod[3:4], mod[4:5], mod[5:6]
    hb = (_rmsnorm(x, gpre_ref[...] * (1.0 + sc2)) + sh2).astype(jnp.bfloat16)

    def conv_cols(lo, n, scale=None):
        cols = slice(lo, lo + n)
        up = _dot(hb, wup_ref[:, cols])
        slabs = range(lo // V7X_LANES, (lo + n) // V7X_LANES)
        _put_slabs(up_buf, CONV_HALO, up, slabs[0])
        u1 = _rows_back(up_buf, CONV_HALO, tile, slabs, 1)
        u2 = _rows_back(up_buf, CONV_HALO, tile, slabs, 2)
        _keep_history(up_buf, CONV_HALO, tile, slabs)
        cw, cb = cw_ref[:, cols], cb_ref[:, cols]
        if scale is not None:
            cw, cb = cw * scale, cb * scale
        return cb + cw[0:1] * u2 + cw[1:2] * u1 + cw[2:3] * up

    ff = None
    for lo, n in chunks:
        gate = conv_cols(lo, n)
        half_val = conv_cols(f + lo, n, scale=0.5)
        p = gate * half_val
        t = jnp.tanh(gate * (GELU_K + (GELU_K * GELU_A) * (gate * gate)))
        act = (p + p * t).astype(jnp.bfloat16)
        part = _dot(act, wdown_ref[lo:lo + n, :])
        ff = part if ff is None else ff + part
    o_ref[0] = x + _rmsnorm(ff, gpost_ref[...] * gt2)


def _ffn(x, mod, g_pre, g_post, w_up, conv_w, conv_b, w_down):
    bsz, seq, d = x.shape
    tile = FFN_TILE
    f = w_down.shape[0]
    assert sum(FFN_CHUNKS) == f
    chunks = tuple((sum(FFN_CHUNKS[:i]), n) for i, n in enumerate(FFN_CHUNKS))
    weights = 2 * (w_up.size + w_down.size)
    up_buf_bytes = 2 * f * (CONV_HALO + tile) * 4
    tok = lambda b, s: (b, s, 0)
    return pl.pallas_call(
        functools.partial(_ffn_kernel, tile=tile, f=f, chunks=chunks),
        out_shape=jax.ShapeDtypeStruct(x.shape, x.dtype),
        grid=(bsz, seq // tile),
        in_specs=[
            pl.BlockSpec((1, tile, d), tok),
            pl.BlockSpec((1, N_MOD, d), lambda b, s: (b, 0, 0)),
            _const_spec((1, d)), _const_spec((1, d)),
            _const_spec(w_up.shape), _const_spec(conv_w.shape), _const_spec((1, 2 * f)),
            _const_spec(w_down.shape),
        ],
        out_specs=pl.BlockSpec((1, tile, d), tok),
        scratch_shapes=[pltpu.VMEM((2 * f // V7X_LANES, CONV_HALO + tile, V7X_LANES), jnp.float32)],
        compiler_params=pltpu.CompilerParams(
            dimension_semantics=("arbitrary", "arbitrary"),
            vmem_limit_bytes=_vmem_limit(weights + up_buf_bytes, 2 * tile * d * 4, 4 * tile * d * 4)),
        name="convffn",
    )(x, mod, g_pre.reshape(1, d), g_post.reshape(1, d), w_up, conv_w, conv_b.reshape(1, 2 * f), w_down)


def kernel(x, c, g_pre_mix, g_post_mix, g_pre_ffn, g_post_ffn, w_ada, b_ada, w_in, w_pool, pool_scale, conv_w,
           conv_b, w_bout, w_o, w_up, ffn_conv_w, ffn_conv_b, w_down):
    bsz, _, d = x.shape
    bf = lambda w: w.astype(jnp.bfloat16)
    for l in range(w_ada.shape[0]):
        mod, w_fold = _ada(c, w_ada[l], b_ada[l], w_in[l], w_pool[l], pool_scale[l])
        mod = mod.reshape(bsz, N_MOD, d)
        x, w_up_bf16, w_down_bf16 = _mixer(x, mod, g_pre_mix[l], g_post_mix[l], w_fold, bf(w_in[l][:, d:]),
                                           conv_w[l], conv_b[l], bf(w_bout[l]), bf(w_o[l]), w_up[l], w_down[l])
        x = _ffn(x, mod, g_pre_ffn[l], g_post_ffn[l], w_up_bf16, ffn_conv_w[l], ffn_conv_b[l], w_down_bf16)
    return x
```

```python
import functools

import jax
import jax.numpy as jnp
from jax import lax
from jax.experimental import pallas as pl
from jax.experimental.pallas import tpu as pltpu

POOL_WINDOWS = (2, 4, 8, 16)
N_MOD = 6
EPS = 1e-6
GELU_K = 0.7978845608028654
GELU_A = 0.044715

V7X_LANES = 128
V7X_BF16_SUBLANES = 16
V7X_VMEM_BYTES = 64 * 1024 * 1024

POOL_HALO = 16
CONV_HALO = 8

MIXER_TILE = 512
FFN_TILE = 512
PREP_STEPS = 8
ROW_SPLIT = 2
FFN_CHUNKS = (1536, 1280)


def _vmem_limit(resident_bytes, streamed_bytes, temp_bytes):
    need = resident_bytes + 2 * streamed_bytes + temp_bytes
    assert need < V7X_VMEM_BYTES, need
    return int(need)


def _const_spec(shape):
    zeros = (0,) * len(shape)
    return pl.BlockSpec(shape, lambda *_: zeros, pipeline_mode=pl.Buffered(1))


def _rmsnorm(xf, gain_row):
    return xf * lax.rsqrt(jnp.mean(xf * xf, axis=-1, keepdims=True) + EPS) * gain_row


def _put_slabs(buf, halo, val, first):
    for i in range(val.shape[1] // V7X_LANES):
        buf[first + i, halo:, :] = val[:, i * V7X_LANES:(i + 1) * V7X_LANES]


def _rows_back(buf, halo, tile, slabs, k):
    return jnp.concatenate([buf[j, halo - k:halo - k + tile, :] for j in slabs], axis=1)


def _keep_history(buf, halo, tile, slabs):
    for j in slabs:
        buf[j, :halo, :] = buf[j, tile:, :]


def _zero_history(buf, halo):
    buf[:, :halo, :] = jnp.zeros((buf.shape[0], halo, V7X_LANES), jnp.float32)


def _dot(a, b):
    return jnp.dot(a, b, preferred_element_type=jnp.float32)


def _row_blocks(rows, steps):
    for n in range(steps, 0, -1):
        if rows % n == 0 and (rows // n) % V7X_BF16_SUBLANES == 0:
            return n, rows // n
    raise ValueError((rows, steps))


def _prep_kernel(c_ref, wada_ref, bada_ref, winpool_ref, wpool_ref, pscale_ref, winrest_ref, wbout_ref, wo_ref,
                 mod_ref, wfold_ref, winrest_bf16, wbout_bf16, wo_bf16):
    mod_ref[...] = _dot(c_ref[...].astype(jnp.bfloat16), wada_ref[...].astype(jnp.bfloat16)) + bada_ref[...]
    wp = wpool_ref[0] * (0.5 * pscale_ref[...])
    wfold = jnp.dot(winpool_ref[...], wp, precision=lax.Precision.HIGHEST, preferred_element_type=jnp.float32)
    wfold_ref[...] = wfold.astype(jnp.bfloat16)
    winrest_bf16[...] = winrest_ref[...].astype(jnp.bfloat16)
    wbout_bf16[...] = wbout_ref[...].astype(jnp.bfloat16)
    wo_bf16[...] = wo_ref[...].astype(jnp.bfloat16)


def _prep(c, w_ada, b_ada, w_in, w_pool, pool_scale, w_bout, w_o):
    bsz, d = c.shape
    n = w_ada.shape[1]
    groups, gw, _ = w_pool.shape
    steps = PREP_STEPS
    half = gw // 2
    assert steps == 2 * groups and half == V7X_LANES and groups * gw == d and n % steps == 0 and d % steps == 0
    ada_cols, rows = n // steps, d // steps
    rest_slabs = w_in.shape[1] // d - 1
    rest = lambda j: (0, jnp.minimum(j, rest_slabs - 1))
    col = lambda j: (0, j)
    row = lambda j: (j, 0)
    streamed = (d * ada_cols * 4 + d * gw * 4 + gw * half * 4 + d * half * 2 + d * d * 6 + 2 * rows * d * 6
                + 4 * bsz * d * 4)
    return pl.pallas_call(
        _prep_kernel,
        out_shape=(jax.ShapeDtypeStruct((bsz, n), jnp.float32), jax.ShapeDtypeStruct((d, d), jnp.bfloat16),
                   jax.ShapeDtypeStruct((d, rest_slabs * d), jnp.bfloat16),
                   jax.ShapeDtypeStruct(w_bout.shape, jnp.bfloat16), jax.ShapeDtypeStruct(w_o.shape, jnp.bfloat16)),
        grid=(steps,),
        in_specs=[
            pl.BlockSpec((bsz, d), lambda j: (0, 0)),
            pl.BlockSpec((d, ada_cols), col),
            pl.BlockSpec((1, ada_cols), col),
            pl.BlockSpec((d, gw), lambda j: (0, j // 2)),
            pl.BlockSpec((1, gw, half), lambda j: (j // 2, 0, j % 2)),
            pl.BlockSpec((1, half), col),
            pl.BlockSpec((d, d), lambda j: (0, jnp.minimum(j, rest_slabs - 1) + 1)),
            pl.BlockSpec((rows, d), row),
            pl.BlockSpec((rows, d), row),
        ],
        out_specs=(pl.BlockSpec((bsz, ada_cols), col), pl.BlockSpec((d, half), col), pl.BlockSpec((d, d), rest),
                   pl.BlockSpec((rows, d), row), pl.BlockSpec((rows, d), row)),
        compiler_params=pltpu.CompilerParams(
            dimension_semantics=("arbitrary",),
            vmem_limit_bytes=_vmem_limit(0, streamed, d * ada_cols * 2 + 4 * d * d * 2)),
        name="prep",
    )(c, w_ada, b_ada.reshape(1, n), w_in, w_pool, pool_scale.reshape(1, d), w_in, w_bout, w_o)


def _mixer_kernel(x_ref, mod_ref, gpre_ref, gpost_ref, wfold_ref, win_ref, convw_ref, convb_ref,
                  wbout_ref, wo_ref, wup_f32, wdown_f32, o_ref, wup_bf16, wdown_bf16, pool_buf, conv_buf,
                  *, tile, d):
    s = pl.program_id(1)
    n_slabs = d // V7X_LANES

    wup_bf16[...] = wup_f32[...].astype(jnp.bfloat16)
    wdown_bf16[...] = wdown_f32[...].astype(jnp.bfloat16)

    @pl.when(s == 0)
    def _():
        _zero_history(pool_buf, POOL_HALO)
        _zero_history(conv_buf, CONV_HALO)

    x = x_ref[0]
    mod = mod_ref[0]
    sh1, sc1, gt1 = mod[0:1], mod[1:2], mod[2:3]
    halves = [slice(i * tile // ROW_SPLIT, (i + 1) * tile // ROW_SPLIT) for i in range(ROW_SPLIT)]
    pre_gain = gpre_ref[...] * (1.0 + sc1)
    hb_parts = [(_rmsnorm(x[rows], pre_gain) + sh1).astype(jnp.bfloat16) for rows in halves]
    hb = jnp.concatenate(hb_parts, axis=0)

    def proj(i):
        return _dot(hb, win_ref[:, (i - 1) * d:i * d])

    u = jnp.concatenate([_dot(part, wfold_ref[...]) for part in hb_parts], axis=0)
    _put_slabs(pool_buf, POOL_HALO, u, 0)
    v = proj(3) * proj(1)
    _put_slabs(conv_buf, CONV_HALO, v, 0)

    gw = d // len(POOL_WINDOWS)
    t1 = (s * tile + lax.broadcasted_iota(jnp.int32, (tile, V7X_LANES), 0) + 1).astype(jnp.float32)
    half_ya = []
    for g, w in enumerate(POOL_WINDOWS):
        slabs = range(g * gw // V7X_LANES, (g + 1) * gw // V7X_LANES)
        ug = u[:, g * gw:(g + 1) * gw]
        acc = ug
        for k in range(1, w):
            acc = acc + _rows_back(pool_buf, POOL_HALO, tile, slabs, k)
        inv = 1.0 / jnp.minimum(t1, float(w))
        inv = jnp.concatenate([inv] * len(slabs), axis=1)
        half_ya.append(acc * inv - ug)
    _keep_history(pool_buf, POOL_HALO, tile, range(n_slabs))
    half_ya = jnp.concatenate(half_ya, axis=1)

    ub = proj(2)

    v1 = _rows_back(conv_buf, CONV_HALO, tile, range(n_slabs), 1)
    v2 = _rows_back(conv_buf, CONV_HALO, tile, range(n_slabs), 2)
    _keep_history(conv_buf, CONV_HALO, tile, range(n_slabs))
    cw, cb = 0.5 * convw_ref[...], 0.5 * convb_ref[...]
    half_conv = cb + cw[0:1] * v2 + cw[1:2] * v1 + cw[2:3] * v
    half_ubc = (ub * half_conv).astype(jnp.bfloat16)

    za = proj(4)
    zb = proj(5)
    gated_a = half_ya + half_ya * jnp.tanh(0.5 * za)
    half_yb = _dot(half_ubc, wbout_ref[...])
    merged = (gated_a + (half_yb + half_yb * jnp.tanh(0.5 * zb))).astype(jnp.bfloat16)
    post_gain = gpost_ref[...] * gt1
    for rows in halves:
        o_ref[0, rows, :] = x[rows] + _rmsnorm(_dot(merged[rows], wo_ref[...]), post_gain)


def _mixer(x, mod, g_pre, g_post, w_fold, w_in_rest, conv_w, conv_b, w_bout, w_o, w_up, w_down):
    bsz, seq, d = x.shape
    tile = MIXER_TILE
    n_seq = seq // tile
    weights = 2 * (w_fold.size + w_in_rest.size + w_bout.size + w_o.size)
    tok = lambda b, s: (b, s, 0)

    def side_spec(w):
        n_blocks, rows = _row_blocks(w.shape[0], bsz * n_seq)
        return pl.BlockSpec((rows, w.shape[1]), lambda b, s: (jnp.minimum(b * n_seq + s, n_blocks - 1), 0)), rows

    (up_spec, up_rows), (down_spec, down_rows) = side_spec(w_up), side_spec(w_down)
    side_bytes = (up_rows * w_up.shape[1] + down_rows * w_down.shape[1]) * (4 + 2)
    hist_bytes = d * (POOL_HALO + CONV_HALO + 2 * tile) * 4
    return pl.pallas_call(
        functools.partial(_mixer_kernel, tile=tile, d=d),
        out_shape=(jax.ShapeDtypeStruct(x.shape, x.dtype),
                   jax.ShapeDtypeStruct(w_up.shape, jnp.bfloat16), jax.ShapeDtypeStruct(w_down.shape, jnp.bfloat16)),
        grid=(bsz, n_seq),
        in_specs=[
            pl.BlockSpec((1, tile, d), tok),
            pl.BlockSpec((1, N_MOD, d), lambda b, s: (b, 0, 0)),
            _const_spec((1, d)), _const_spec((1, d)),
            _const_spec(w_fold.shape), _const_spec(w_in_rest.shape),
            _const_spec(conv_w.shape), _const_spec((1, d)),
            _const_spec(w_bout.shape), _const_spec(w_o.shape),
            up_spec, down_spec,
        ],
        out_specs=(pl.BlockSpec((1, tile, d), tok), up_spec, down_spec),
        scratch_shapes=[pltpu.VMEM((d // V7X_LANES, POOL_HALO + tile, V7X_LANES), jnp.float32),
                        pltpu.VMEM((d // V7X_LANES, CONV_HALO + tile, V7X_LANES), jnp.float32)],
        compiler_params=pltpu.CompilerParams(
            dimension_semantics=("arbitrary", "arbitrary"),
            vmem_limit_bytes=_vmem_limit(weights + hist_bytes, 2 * tile * d * 4 + side_bytes, 4 * tile * d * 4)),
        name="mixer",
    )(x, mod, g_pre.reshape(1, d), g_post.reshape(1, d), w_fold, w_in_rest,
      conv_w, conv_b.reshape(1, d), w_bout, w_o, w_up, w_down)


def _ffn_kernel(x_ref, mod_ref, gpre_ref, gpost_ref, wup_ref, cw_ref, cb_ref, wdown_ref, o_ref, up_buf,
                *, tile, f, chunks):
    s = pl.program_id(1)

    @pl.when(s == 0)
    def _():
        _zero_history(up_buf, CONV_HALO)

    x = x_ref[0]
    mod = mod_ref[0]
    sh2, sc2, gt2 = mod[3:4], mod[4:5], mod[5:6]
    hb = (_rmsnorm(x, gpre_ref[...] * (1.0 + sc2)) + sh2).astype(jnp.bfloat16)

    def conv_cols(lo, n, scale=None):
        cols = slice(lo, lo + n)
        up = _dot(hb, wup_ref[:, cols])
        slabs = range(lo // V7X_LANES, (lo + n) // V7X_LANES)
        _put_slabs(up_buf, CONV_HALO, up, slabs[0])
        u1 = _rows_back(up_buf, CONV_HALO, tile, slabs, 1)
        u2 = _rows_back(up_buf, CONV_HALO, tile, slabs, 2)
        _keep_history(up_buf, CONV_HALO, tile, slabs)
        cw, cb = cw_ref[:, cols], cb_ref[:, cols]
        if scale is not None:
            cw, cb = cw * scale, cb * scale
        return cb + cw[0:1] * u2 + cw[1:2] * u1 + cw[2:3] * up

    ff = None
    for lo, n in chunks:
        gate = conv_cols(lo, n)
        half_val = conv_cols(f + lo, n, scale=0.5)
        p = gate * half_val
        t = jnp.tanh(gate * (GELU_K + (GELU_K * GELU_A) * (gate * gate)))
        act = (p + p * t).astype(jnp.bfloat16)
        part = _dot(act, wdown_ref[lo:lo + n, :])
        ff = part if ff is None else ff + part
    o_ref[0] = x + _rmsnorm(ff, gpost_ref[...] * gt2)


def _ffn(x, mod, g_pre, g_post, w_up, conv_w, conv_b, w_down):
    bsz, seq, d = x.shape
    tile = FFN_TILE
    f = w_down.shape[0]
    assert sum(FFN_CHUNKS) == f
    chunks = tuple((sum(FFN_CHUNKS[:i]), n) for i, n in enumerate(FFN_CHUNKS))
    weights = 2 * (w_up.size + w_down.size)
    up_buf_bytes = 2 * f * (CONV_HALO + tile) * 4
    tok = lambda b, s: (b, s, 0)
    return pl.pallas_call(
        functools.partial(_ffn_kernel, tile=tile, f=f, chunks=chunks),
        out_shape=jax.ShapeDtypeStruct(x.shape, x.dtype),
        grid=(bsz, seq // tile),
        in_specs=[
            pl.BlockSpec((1, tile, d), tok),
            pl.BlockSpec((1, N_MOD, d), lambda b, s: (b, 0, 0)),
            _const_spec((1, d)), _const_spec((1, d)),
            _const_spec(w_up.shape), _const_spec(conv_w.shape), _const_spec((1, 2 * f)),
            _const_spec(w_down.shape),
        ],
        out_specs=pl.BlockSpec((1, tile, d), tok),
        scratch_shapes=[pltpu.VMEM((2 * f // V7X_LANES, CONV_HALO + tile, V7X_LANES), jnp.float32)],
        compiler_params=pltpu.CompilerParams(
            dimension_semantics=("arbitrary", "arbitrary"),
            vmem_limit_bytes=_vmem_limit(weights + up_buf_bytes, 2 * tile * d * 4, 4 * tile * d * 4)),
        name="convffn",
    )(x, mod, g_pre.reshape(1, d), g_post.reshape(1, d), w_up, conv_w, conv_b.reshape(1, 2 * f), w_down)


def kernel(x, c, g_pre_mix, g_post_mix, g_pre_ffn, g_post_ffn, w_ada, b_ada, w_in, w_pool, pool_scale, conv_w,
           conv_b, w_bout, w_o, w_up, ffn_conv_w, ffn_conv_b, w_down):
    bsz, _, d = x.shape
    for l in range(w_ada.shape[0]):
        mod, w_fold, w_in_rest, w_bout_bf16, w_o_bf16 = _prep(c, w_ada[l], b_ada[l], w_in[l], w_pool[l],
                                                              pool_scale[l], w_bout[l], w_o[l])
        mod = mod.reshape(bsz, N_MOD, d)
        x, w_up_bf16, w_down_bf16 = _mixer(x, mod, g_pre_mix[l], g_post_mix[l], w_fold, w_in_rest,
                                           conv_w[l], conv_b[l], w_bout_bf16, w_o_bf16, w_up[l], w_down[l])
        x = _ffn(x, mod, g_pre_ffn[l], g_post_ffn[l], w_up_bf16, ffn_conv_w[l], ffn_conv_b[l], w_down_bf16)
    return x
```

```python
import functools

import jax
import jax.numpy as jnp
from jax import lax
from jax.experimental import pallas as pl
from jax.experimental.pallas import tpu as pltpu

POOL_WINDOWS = (2, 4, 8, 16)
N_MOD = 6
EPS = 1e-6
GELU_K = 0.7978845608028654
GELU_A = 0.044715

V7X_LANES = 128
V7X_BF16_SUBLANES = 16
V7X_VMEM_BYTES = 64 * 1024 * 1024

POOL_HALO = 16
CONV_HALO = 8

MIXER_TILE = 512
FFN_TILE = 512
PREP_STEPS = 8
WEIGHT_LOAD_COLS = 512
ROW_SPLIT = 2
FFN_CHUNKS = (1536, 1280)


def _vmem_limit(resident_bytes, streamed_bytes, temp_bytes):
    need = resident_bytes + 2 * streamed_bytes + temp_bytes
    assert need < V7X_VMEM_BYTES, need
    return int(need)


def _const_spec(shape):
    zeros = (0,) * len(shape)
    return pl.BlockSpec(shape, lambda *_: zeros, pipeline_mode=pl.Buffered(1))


def _rmsnorm(xf, gain_row):
    return xf * lax.rsqrt(jnp.mean(xf * xf, axis=-1, keepdims=True) + EPS) * gain_row


def _put_slabs(buf, halo, val, first):
    for i in range(val.shape[1] // V7X_LANES):
        buf[first + i, halo:, :] = val[:, i * V7X_LANES:(i + 1) * V7X_LANES]


def _rows_back(buf, halo, tile, slabs, k):
    return jnp.concatenate([buf[j, halo - k:halo - k + tile, :] for j in slabs], axis=1)


def _keep_history(buf, halo, tile, slabs):
    for j in slabs:
        buf[j, :halo, :] = buf[j, tile:, :]


def _zero_history(buf, halo):
    buf[:, :halo, :] = jnp.zeros((buf.shape[0], halo, V7X_LANES), jnp.float32)


def _dot(a, b):
    return jnp.dot(a, b, preferred_element_type=jnp.float32)


def _row_blocks(rows, steps):
    for n in range(steps, 0, -1):
        if rows % n == 0 and (rows // n) % V7X_BF16_SUBLANES == 0:
            return n, rows // n
    raise ValueError((rows, steps))


def _prep_kernel(c_ref, wada_ref, bada_ref, winpool_ref, wpool_ref, pscale_ref, mod_ref, wfold_ref):
    mod_ref[...] = _dot(c_ref[...].astype(jnp.bfloat16), wada_ref[...].astype(jnp.bfloat16)) + bada_ref[...]
    wp = wpool_ref[0] * (0.5 * pscale_ref[...])
    wfold = jnp.dot(winpool_ref[...], wp, precision=lax.Precision.HIGHEST, preferred_element_type=jnp.float32)
    wfold_ref[...] = wfold.astype(jnp.bfloat16)


def _prep(c, w_ada, b_ada, w_in, w_pool, pool_scale):
    bsz, d = c.shape
    n = w_ada.shape[1]
    groups, gw, _ = w_pool.shape
    steps = PREP_STEPS
    half = gw // 2
    assert steps == 2 * groups and half == V7X_LANES and groups * gw == d and n % steps == 0
    ada_cols = n // steps
    col = lambda j: (0, j)
    streamed = d * ada_cols * 4 + d * gw * 4 + gw * half * 4 + d * half * 2 + 4 * bsz * d * 4
    return pl.pallas_call(
        _prep_kernel,
        out_shape=(jax.ShapeDtypeStruct((bsz, n), jnp.float32), jax.ShapeDtypeStruct((d, d), jnp.bfloat16)),
        grid=(steps,),
        in_specs=[
            pl.BlockSpec((bsz, d), lambda j: (0, 0)),
            pl.BlockSpec((d, ada_cols), col),
            pl.BlockSpec((1, ada_cols), col),
            pl.BlockSpec((d, gw), lambda j: (0, j // 2)),
            pl.BlockSpec((1, gw, half), lambda j: (j // 2, 0, j % 2)),
            pl.BlockSpec((1, half), col),
        ],
        out_specs=(pl.BlockSpec((bsz, ada_cols), col), pl.BlockSpec((d, half), col)),
        compiler_params=pltpu.CompilerParams(
            dimension_semantics=("arbitrary",),
            vmem_limit_bytes=_vmem_limit(0, streamed, d * ada_cols * 2 + 8 * d * gw * 4)),
        name="prep",
    )(c, w_ada, b_ada.reshape(1, n), w_in, w_pool, pool_scale.reshape(1, d))


def _mixer_kernel(x_ref, mod_ref, gpre_ref, gpost_ref, wfold_ref, convw_ref, convb_ref, wup_f32, wdown_f32,
                  win_hbm, wbout_hbm, wo_hbm, o_ref, wup_bf16, wdown_bf16,
                  pool_buf, conv_buf, win_ref, wbout_ref, wo_ref, stage, stage_sem, *, tile, d):
    s = pl.program_id(1)
    n_slabs = d // V7X_LANES

    @pl.when((pl.program_id(0) == 0) & (s == 0))
    def _():
        cols = stage.shape[2]
        moves = [(win_hbm.at[:, pl.ds(c, cols)], win_ref.at[:, pl.ds(c - d, cols)])
                 for c in range(d, win_hbm.shape[1], cols)]
        for src, dst in ((wbout_hbm, wbout_ref), (wo_hbm, wo_ref)):
            moves += [(src.at[:, pl.ds(c, cols)], dst.at[:, pl.ds(c, cols)]) for c in range(0, d, cols)]

        def load(i):
            return pltpu.make_async_copy(moves[i][0], stage.at[i % 2], stage_sem.at[i % 2])

        load(0).start()
        for i in range(len(moves)):
            if i + 1 < len(moves):
                load(i + 1).start()
            load(i).wait()
            moves[i][1][...] = stage[i % 2].astype(jnp.bfloat16)

    wup_bf16[...] = wup_f32[...].astype(jnp.bfloat16)
    wdown_bf16[...] = wdown_f32[...].astype(jnp.bfloat16)

    @pl.when(s == 0)
    def _():
        _zero_history(pool_buf, POOL_HALO)
        _zero_history(conv_buf, CONV_HALO)

    x = x_ref[0]
    mod = mod_ref[0]
    sh1, sc1, gt1 = mod[0:1], mod[1:2], mod[2:3]
    halves = [slice(i * tile // ROW_SPLIT, (i + 1) * tile // ROW_SPLIT) for i in range(ROW_SPLIT)]
    pre_gain = gpre_ref[...] * (1.0 + sc1)
    hb_parts = [(_rmsnorm(x[rows], pre_gain) + sh1).astype(jnp.bfloat16) for rows in halves]
    hb = jnp.concatenate(hb_parts, axis=0)

    def proj(i):
        return _dot(hb, win_ref[:, (i - 1) * d:i * d])

    u = jnp.concatenate([_dot(part, wfold_ref[...]) for part in hb_parts], axis=0)
    _put_slabs(pool_buf, POOL_HALO, u, 0)
    v = proj(3) * proj(1)
    _put_slabs(conv_buf, CONV_HALO, v, 0)

    gw = d // len(POOL_WINDOWS)
    t1 = (s * tile + lax.broadcasted_iota(jnp.int32, (tile, V7X_LANES), 0) + 1).astype(jnp.float32)
    half_ya = []
    for g, w in enumerate(POOL_WINDOWS):
        slabs = range(g * gw // V7X_LANES, (g + 1) * gw // V7X_LANES)
        ug = u[:, g * gw:(g + 1) * gw]
        acc = ug
        for k in range(1, w):
            acc = acc + _rows_back(pool_buf, POOL_HALO, tile, slabs, k)
        inv = 1.0 / jnp.minimum(t1, float(w))
        inv = jnp.concatenate([inv] * len(slabs), axis=1)
        half_ya.append(acc * inv - ug)
    _keep_history(pool_buf, POOL_HALO, tile, range(n_slabs))
    half_ya = jnp.concatenate(half_ya, axis=1)

    ub = proj(2)

    v1 = _rows_back(conv_buf, CONV_HALO, tile, range(n_slabs), 1)
    v2 = _rows_back(conv_buf, CONV_HALO, tile, range(n_slabs), 2)
    _keep_history(conv_buf, CONV_HALO, tile, range(n_slabs))
    cw, cb = 0.5 * convw_ref[...], 0.5 * convb_ref[...]
    half_conv = cb + cw[0:1] * v2 + cw[1:2] * v1 + cw[2:3] * v
    half_ubc = (ub * half_conv).astype(jnp.bfloat16)

    za = proj(4)
    zb = proj(5)
    gated_a = half_ya + half_ya * jnp.tanh(0.5 * za)
    half_yb = _dot(half_ubc, wbout_ref[...])
    merged = (gated_a + (half_yb + half_yb * jnp.tanh(0.5 * zb))).astype(jnp.bfloat16)
    post_gain = gpost_ref[...] * gt1
    for rows in halves:
        o_ref[0, rows, :] = x[rows] + _rmsnorm(_dot(merged[rows], wo_ref[...]), post_gain)


def _mixer(x, mod, g_pre, g_post, w_fold, w_in, conv_w, conv_b, w_bout, w_o, w_up, w_down):
    bsz, seq, d = x.shape
    tile = MIXER_TILE
    n_seq = seq // tile
    d_rest = w_in.shape[1] - d
    assert d_rest % WEIGHT_LOAD_COLS == 0 and d % WEIGHT_LOAD_COLS == 0
    weights = 2 * (w_fold.size + d * d_rest + w_bout.size + w_o.size) + 2 * d * WEIGHT_LOAD_COLS * 4
    tok = lambda b, s: (b, s, 0)
    hbm = pl.BlockSpec(memory_space=pl.ANY)

    def side_spec(w):
        n_blocks, rows = _row_blocks(w.shape[0], bsz * n_seq)
        return pl.BlockSpec((rows, w.shape[1]), lambda b, s: (jnp.minimum(b * n_seq + s, n_blocks - 1), 0)), rows

    (up_spec, up_rows), (down_spec, down_rows) = side_spec(w_up), side_spec(w_down)
    side_bytes = (up_rows * w_up.shape[1] + down_rows * w_down.shape[1]) * (4 + 2)
    hist_bytes = d * (POOL_HALO + CONV_HALO + 2 * tile) * 4
    return pl.pallas_call(
        functools.partial(_mixer_kernel, tile=tile, d=d),
        out_shape=(jax.ShapeDtypeStruct(x.shape, x.dtype),
                   jax.ShapeDtypeStruct(w_up.shape, jnp.bfloat16), jax.ShapeDtypeStruct(w_down.shape, jnp.bfloat16)),
        grid=(bsz, n_seq),
        in_specs=[
            pl.BlockSpec((1, tile, d), tok),
            pl.BlockSpec((1, N_MOD, d), lambda b, s: (b, 0, 0)),
            _const_spec((1, d)), _const_spec((1, d)),
            _const_spec(w_fold.shape),
            _const_spec(conv_w.shape), _const_spec((1, d)),
            up_spec, down_spec,
            hbm, hbm, hbm,
        ],
        out_specs=(pl.BlockSpec((1, tile, d), tok), up_spec, down_spec),
        scratch_shapes=[pltpu.VMEM((d // V7X_LANES, POOL_HALO + tile, V7X_LANES), jnp.float32),
                        pltpu.VMEM((d // V7X_LANES, CONV_HALO + tile, V7X_LANES), jnp.float32),
                        pltpu.VMEM((d, d_rest), jnp.bfloat16), pltpu.VMEM(w_bout.shape, jnp.bfloat16),
                        pltpu.VMEM(w_o.shape, jnp.bfloat16),
                        pltpu.VMEM((2, d, WEIGHT_LOAD_COLS), jnp.float32), pltpu.SemaphoreType.DMA((2,))],
        compiler_params=pltpu.CompilerParams(
            dimension_semantics=("arbitrary", "arbitrary"),
            vmem_limit_bytes=_vmem_limit(weights + hist_bytes, 2 * tile * d * 4 + side_bytes, 4 * tile * d * 4)),
        name="mixer",
    )(x, mod, g_pre.reshape(1, d), g_post.reshape(1, d), w_fold, conv_w, conv_b.reshape(1, d), w_up, w_down,
      w_in, w_bout, w_o)


def _ffn_kernel(x_ref, mod_ref, gpre_ref, gpost_ref, wup_ref, cw_ref, cb_ref, wdown_ref, o_ref, up_buf,
                *, tile, f, chunks):
    s = pl.program_id(1)

    @pl.when(s == 0)
    def _():
        _zero_history(up_buf, CONV_HALO)

    x = x_ref[0]
    mod = mod_ref[0]
    sh2, sc2, gt2 = mod[3:4], mod[4:5], mod[5:6]
    hb = (_rmsnorm(x, gpre_ref[...] * (1.0 + sc2)) + sh2).astype(jnp.bfloat16)

    def conv_cols(lo, n, scale=None):
        cols = slice(lo, lo + n)
        up = _dot(hb, wup_ref[:, cols])
        slabs = range(lo // V7X_LANES, (lo + n) // V7X_LANES)
        _put_slabs(up_buf, CONV_HALO, up, slabs[0])
        u1 = _rows_back(up_buf, CONV_HALO, tile, slabs, 1)
        u2 = _rows_back(up_buf, CONV_HALO, tile, slabs, 2)
        _keep_history(up_buf, CONV_HALO, tile, slabs)
        cw, cb = cw_ref[:, cols], cb_ref[:, cols]
        if scale is not None:
            cw, cb = cw * scale, cb * scale
        return cb + cw[0:1] * u2 + cw[1:2] * u1 + cw[2:3] * up

    ff = None
    for lo, n in chunks:
        gate = conv_cols(lo, n)
        half_val = conv_cols(f + lo, n, scale=0.5)
        p = gate * half_val
        t = jnp.tanh(gate * (GELU_K + (GELU_K * GELU_A) * (gate * gate)))
        act = (p + p * t).astype(jnp.bfloat16)
        part = _dot(act, wdown_ref[lo:lo + n, :])
        ff = part if ff is None else ff + part
    o_ref[0] = x + _rmsnorm(ff, gpost_ref[...] * gt2)


def _ffn(x, mod, g_pre, g_post, w_up, conv_w, conv_b, w_down):
    bsz, seq, d = x.shape
    tile = FFN_TILE
    f = w_down.shape[0]
    assert sum(FFN_CHUNKS) == f
    chunks = tuple((sum(FFN_CHUNKS[:i]), n) for i, n in enumerate(FFN_CHUNKS))
    weights = 2 * (w_up.size + w_down.size)
    up_buf_bytes = 2 * f * (CONV_HALO + tile) * 4
    tok = lambda b, s: (b, s, 0)
    return pl.pallas_call(
        functools.partial(_ffn_kernel, tile=tile, f=f, chunks=chunks),
        out_shape=jax.ShapeDtypeStruct(x.shape, x.dtype),
        grid=(bsz, seq // tile),
        in_specs=[
            pl.BlockSpec((1, tile, d), tok),
            pl.BlockSpec((1, N_MOD, d), lambda b, s: (b, 0, 0)),
            _const_spec((1, d)), _const_spec((1, d)),
            _const_spec(w_up.shape), _const_spec(conv_w.shape), _const_spec((1, 2 * f)),
            _const_spec(w_down.shape),
        ],
        out_specs=pl.BlockSpec((1, tile, d), tok),
        scratch_shapes=[pltpu.VMEM((2 * f // V7X_LANES, CONV_HALO + tile, V7X_LANES), jnp.float32)],
        compiler_params=pltpu.CompilerParams(
            dimension_semantics=("arbitrary", "arbitrary"),
            vmem_limit_bytes=_vmem_limit(weights + up_buf_bytes, 2 * tile * d * 4, 4 * tile * d * 4)),
        name="convffn",
    )(x, mod, g_pre.reshape(1, d), g_post.reshape(1, d), w_up, conv_w, conv_b.reshape(1, 2 * f), w_down)


def kernel(x, c, g_pre_mix, g_post_mix, g_pre_ffn, g_post_ffn, w_ada, b_ada, w_in, w_pool, pool_scale, conv_w,
           conv_b, w_bout, w_o, w_up, ffn_conv_w, ffn_conv_b, w_down):
    bsz, _, d = x.shape
    for l in range(w_ada.shape[0]):
        mod, w_fold = _prep(c, w_ada[l], b_ada[l], w_in[l], w_pool[l], pool_scale[l])
        mod = mod.reshape(bsz, N_MOD, d)
        x, w_up_bf16, w_down_bf16 = _mixer(x, mod, g_pre_mix[l], g_post_mix[l], w_fold, w_in[l],
                                           conv_w[l], conv_b[l], w_bout[l], w_o[l], w_up[l], w_down[l])
        x = _ffn(x, mod, g_pre_ffn[l], g_post_ffn[l], w_up_bf16, ffn_conv_w[l], ffn_conv_b[l], w_down_bf16)
    return x
```

```python
import functools

import jax
import jax.numpy as jnp
from jax import lax
from jax.experimental import pallas as pl
from jax.experimental.pallas import tpu as pltpu

POOL_WINDOWS = (2, 4, 8, 16)
N_MOD = 6
EPS = 1e-6
GELU_K = 0.7978845608028654
GELU_A = 0.044715

V7X_LANES = 128
V7X_BF16_SUBLANES = 16
V7X_VMEM_BYTES = 64 * 1024 * 1024

POOL_HALO = 16
CONV_HALO = 8

MIXER_TILE = 512
FFN_TILE = 512
PREP_STEPS = 8
PREP_VMEM_CLAIM = 48 * 1024 * 1024
WEIGHT_LOAD_COLS = 512
ROW_SPLIT = 2
FFN_CHUNKS = (1536, 1280)


def _vmem_limit(resident_bytes, streamed_bytes, temp_bytes):
    need = resident_bytes + 2 * streamed_bytes + temp_bytes
    assert need < V7X_VMEM_BYTES, need
    return int(need)


def _const_spec(shape):
    zeros = (0,) * len(shape)
    return pl.BlockSpec(shape, lambda *_: zeros, pipeline_mode=pl.Buffered(1))


def _rmsnorm(xf, gain_row):
    return xf * lax.rsqrt(jnp.mean(xf * xf, axis=-1, keepdims=True) + EPS) * gain_row


def _put_slabs(buf, halo, val, first):
    for i in range(val.shape[1] // V7X_LANES):
        buf[first + i, halo:, :] = val[:, i * V7X_LANES:(i + 1) * V7X_LANES]


def _rows_back(buf, halo, tile, slabs, k):
    return jnp.concatenate([buf[j, halo - k:halo - k + tile, :] for j in slabs], axis=1)


def _keep_history(buf, halo, tile, slabs):
    for j in slabs:
        buf[j, :halo, :] = buf[j, tile:, :]


def _zero_history(buf, halo):
    buf[:, :halo, :] = jnp.zeros((buf.shape[0], halo, V7X_LANES), jnp.float32)


def _dot(a, b):
    return jnp.dot(a, b, preferred_element_type=jnp.float32)


def _row_blocks(rows, steps):
    for n in range(steps, 0, -1):
        if rows % n == 0 and (rows // n) % V7X_BF16_SUBLANES == 0:
            return n, rows // n
    raise ValueError((rows, steps))


def _prep_kernel(c_ref, wada_ref, bada_ref, winpool_ref, wpool_ref, pscale_ref, mod_ref, wfold_ref):
    mod_ref[...] = _dot(c_ref[...].astype(jnp.bfloat16), wada_ref[...].astype(jnp.bfloat16)) + bada_ref[...]
    wp = wpool_ref[0] * (0.5 * pscale_ref[...])
    wfold = jnp.dot(winpool_ref[...], wp, precision=lax.Precision.HIGHEST, preferred_element_type=jnp.float32)
    wfold_ref[...] = wfold.astype(jnp.bfloat16)


def _prep(c, w_ada, b_ada, w_in, w_pool, pool_scale):
    bsz, d = c.shape
    n = w_ada.shape[1]
    groups, gw, _ = w_pool.shape
    steps = PREP_STEPS
    half = gw // 2
    assert steps == 2 * groups and half == V7X_LANES and groups * gw == d and n % steps == 0
    ada_cols = n // steps
    col = lambda j: (0, j)
    streamed = d * ada_cols * 4 + d * gw * 4 + gw * half * 4 + d * half * 2 + 4 * bsz * d * 4
    return pl.pallas_call(
        _prep_kernel,
        out_shape=(jax.ShapeDtypeStruct((bsz, n), jnp.float32), jax.ShapeDtypeStruct((d, d), jnp.bfloat16)),
        grid=(steps,),
        in_specs=[
            pl.BlockSpec((bsz, d), lambda j: (0, 0)),
            pl.BlockSpec((d, ada_cols), col),
            pl.BlockSpec((1, ada_cols), col),
            pl.BlockSpec((d, gw), lambda j: (0, j // 2)),
            pl.BlockSpec((1, gw, half), lambda j: (j // 2, 0, j % 2)),
            pl.BlockSpec((1, half), col),
        ],
        out_specs=(pl.BlockSpec((bsz, ada_cols), col), pl.BlockSpec((d, half), col)),
        compiler_params=pltpu.CompilerParams(
            dimension_semantics=("arbitrary",),
            vmem_limit_bytes=max(_vmem_limit(0, streamed, d * ada_cols * 2 + 8 * d * gw * 4), PREP_VMEM_CLAIM)),
        name="prep",
    )(c, w_ada, b_ada.reshape(1, n), w_in, w_pool, pool_scale.reshape(1, d))


def _mixer_kernel(x_ref, mod_ref, gpre_ref, gpost_ref, wfold_ref, convw_ref, convb_ref, wup_f32, wdown_f32,
                  win_hbm, wbout_hbm, wo_hbm, o_ref, wup_bf16, wdown_bf16,
                  pool_buf, conv_buf, win_ref, wbout_ref, wo_ref, stage, stage_sem, *, tile, d):
    s = pl.program_id(1)
    n_slabs = d // V7X_LANES

    @pl.when((pl.program_id(0) == 0) & (s == 0))
    def _():
        cols = stage.shape[2]
        moves = [(win_hbm.at[:, pl.ds(c, cols)], win_ref.at[:, pl.ds(c - d, cols)])
                 for c in range(d, win_hbm.shape[1], cols)]
        for src, dst in ((wbout_hbm, wbout_ref), (wo_hbm, wo_ref)):
            moves += [(src.at[:, pl.ds(c, cols)], dst.at[:, pl.ds(c, cols)]) for c in range(0, d, cols)]

        def load(i):
            return pltpu.make_async_copy(moves[i][0], stage.at[i % 2], stage_sem.at[i % 2])

        load(0).start()
        for i in range(len(moves)):
            if i + 1 < len(moves):
                load(i + 1).start()
            load(i).wait()
            moves[i][1][...] = stage[i % 2].astype(jnp.bfloat16)

    wup_bf16[...] = wup_f32[...].astype(jnp.bfloat16)
    wdown_bf16[...] = wdown_f32[...].astype(jnp.bfloat16)

    @pl.when(s == 0)
    def _():
        _zero_history(pool_buf, POOL_HALO)
        _zero_history(conv_buf, CONV_HALO)

    x = x_ref[0]
    mod = mod_ref[0]
    sh1, sc1, gt1 = mod[0:1], mod[1:2], mod[2:3]
    halves = [slice(i * tile // ROW_SPLIT, (i + 1) * tile // ROW_SPLIT) for i in range(ROW_SPLIT)]
    pre_gain = gpre_ref[...] * (1.0 + sc1)
    hb_parts = [(_rmsnorm(x[rows], pre_gain) + sh1).astype(jnp.bfloat16) for rows in halves]
    hb = jnp.concatenate(hb_parts, axis=0)

    def proj(i):
        return _dot(hb, win_ref[:, (i - 1) * d:i * d])

    u = jnp.concatenate([_dot(part, wfold_ref[...]) for part in hb_parts], axis=0)
    _put_slabs(pool_buf, POOL_HALO, u, 0)
    v = proj(3) * proj(1)
    _put_slabs(conv_buf, CONV_HALO, v, 0)

    gw = d // len(POOL_WINDOWS)
    t1 = (s * tile + lax.broadcasted_iota(jnp.int32, (tile, V7X_LANES), 0) + 1).astype(jnp.float32)
    half_ya = []
    for g, w in enumerate(POOL_WINDOWS):
        slabs = range(g * gw // V7X_LANES, (g + 1) * gw // V7X_LANES)
        ug = u[:, g * gw:(g + 1) * gw]
        acc = ug
        for k in range(1, w):
            acc = acc + _rows_back(pool_buf, POOL_HALO, tile, slabs, k)
        inv = 1.0 / jnp.minimum(t1, float(w))
        inv = jnp.concatenate([inv] * len(slabs), axis=1)
        half_ya.append(acc * inv - ug)
    _keep_history(pool_buf, POOL_HALO, tile, range(n_slabs))
    half_ya = jnp.concatenate(half_ya, axis=1)

    ub = proj(2)

    v1 = _rows_back(conv_buf, CONV_HALO, tile, range(n_slabs), 1)
    v2 = _rows_back(conv_buf, CONV_HALO, tile, range(n_slabs), 2)
    _keep_history(conv_buf, CONV_HALO, tile, range(n_slabs))
    cw, cb = 0.5 * convw_ref[...], 0.5 * convb_ref[...]
    half_conv = cb + cw[0:1] * v2 + cw[1:2] * v1 + cw[2:3] * v
    half_ubc = (ub * half_conv).astype(jnp.bfloat16)

    za = proj(4)
    zb = proj(5)
    gated_a = half_ya + half_ya * jnp.tanh(0.5 * za)
    half_yb = _dot(half_ubc, wbout_ref[...])
    merged = (gated_a + (half_yb + half_yb * jnp.tanh(0.5 * zb))).astype(jnp.bfloat16)
    post_gain = gpost_ref[...] * gt1
    for rows in halves:
        o_ref[0, rows, :] = x[rows] + _rmsnorm(_dot(merged[rows], wo_ref[...]), post_gain)


def _mixer(x, mod, g_pre, g_post, w_fold, w_in, conv_w, conv_b, w_bout, w_o, w_up, w_down):
    bsz, seq, d = x.shape
    tile = MIXER_TILE
    n_seq = seq // tile
    d_rest = w_in.shape[1] - d
    assert d_rest % WEIGHT_LOAD_COLS == 0 and d % WEIGHT_LOAD_COLS == 0
    weights = 2 * (w_fold.size + d * d_rest + w_bout.size + w_o.size) + 2 * d * WEIGHT_LOAD_COLS * 4
    tok = lambda b, s: (b, s, 0)
    hbm = pl.BlockSpec(memory_space=pl.ANY)

    def side_spec(w):
        n_blocks, rows = _row_blocks(w.shape[0], bsz * n_seq)
        return pl.BlockSpec((rows, w.shape[1]), lambda b, s: (jnp.minimum(b * n_seq + s, n_blocks - 1), 0)), rows

    (up_spec, up_rows), (down_spec, down_rows) = side_spec(w_up), side_spec(w_down)
    side_bytes = (up_rows * w_up.shape[1] + down_rows * w_down.shape[1]) * (4 + 2)
    hist_bytes = d * (POOL_HALO + CONV_HALO + 2 * tile) * 4
    return pl.pallas_call(
        functools.partial(_mixer_kernel, tile=tile, d=d),
        out_shape=(jax.ShapeDtypeStruct(x.shape, x.dtype),
                   jax.ShapeDtypeStruct(w_up.shape, jnp.bfloat16), jax.ShapeDtypeStruct(w_down.shape, jnp.bfloat16)),
        grid=(bsz, n_seq),
        in_specs=[
            pl.BlockSpec((1, tile, d), tok),
            pl.BlockSpec((1, N_MOD, d), lambda b, s: (b, 0, 0)),
            _const_spec((1, d)), _const_spec((1, d)),
            _const_spec(w_fold.shape),
            _const_spec(conv_w.shape), _const_spec((1, d)),
            up_spec, down_spec,
            hbm, hbm, hbm,
        ],
        out_specs=(pl.BlockSpec((1, tile, d), tok), up_spec, down_spec),
        scratch_shapes=[pltpu.VMEM((d // V7X_LANES, POOL_HALO + tile, V7X_LANES), jnp.float32),
                        pltpu.VMEM((d // V7X_LANES, CONV_HALO + tile, V7X_LANES), jnp.float32),
                        pltpu.VMEM((d, d_rest), jnp.bfloat16), pltpu.VMEM(w_bout.shape, jnp.bfloat16),
                        pltpu.VMEM(w_o.shape, jnp.bfloat16),
                        pltpu.VMEM((2, d, WEIGHT_LOAD_COLS), jnp.float32), pltpu.SemaphoreType.DMA((2,))],
        compiler_params=pltpu.CompilerParams(
            dimension_semantics=("arbitrary", "arbitrary"),
            vmem_limit_bytes=_vmem_limit(weights + hist_bytes, 2 * tile * d * 4 + side_bytes, 4 * tile * d * 4)),
        name="mixer",
    )(x, mod, g_pre.reshape(1, d), g_post.reshape(1, d), w_fold, conv_w, conv_b.reshape(1, d), w_up, w_down,
      w_in, w_bout, w_o)


def _ffn_kernel(x_ref, mod_ref, gpre_ref, gpost_ref, wup_ref, cw_ref, cb_ref, wdown_ref, o_ref, up_buf,
                *, tile, f, chunks):
    s = pl.program_id(1)

    @pl.when(s == 0)
    def _():
        _zero_history(up_buf, CONV_HALO)

    x = x_ref[0]
    mod = mod_ref[0]
    sh2, sc2, gt2 = mod[3:4], mod[4:5], mod[5:6]
    hb = (_rmsnorm(x, gpre_ref[...] * (1.0 + sc2)) + sh2).astype(jnp.bfloat16)

    def conv_cols(lo, n, scale=None):
        cols = slice(lo, lo + n)
        up = _dot(hb, wup_ref[:, cols])
        slabs = range(lo // V7X_LANES, (lo + n) // V7X_LANES)
        _put_slabs(up_buf, CONV_HALO, up, slabs[0])
        u1 = _rows_back(up_buf, CONV_HALO, tile, slabs, 1)
        u2 = _rows_back(up_buf, CONV_HALO, tile, slabs, 2)
        _keep_history(up_buf, CONV_HALO, tile, slabs)
        cw, cb = cw_ref[:, cols], cb_ref[:, cols]
        if scale is not None:
            cw, cb = cw * scale, cb * scale
        return cb + cw[0:1] * u2 + cw[1:2] * u1 + cw[2:3] * up

    ff = None
    for lo, n in chunks:
        gate = conv_cols(lo, n)
        half_val = conv_cols(f + lo, n, scale=0.5)
        p = gate * half_val
        t = jnp.tanh(gate * (GELU_K + (GELU_K * GELU_A) * (gate * gate)))
        act = (p + p * t).astype(jnp.bfloat16)
        part = _dot(act, wdown_ref[lo:lo + n, :])
        ff = part if ff is None else ff + part
    o_ref[0] = x + _rmsnorm(ff, gpost_ref[...] * gt2)


def _ffn(x, mod, g_pre, g_post, w_up, conv_w, conv_b, w_down):
    bsz, seq, d = x.shape
    tile = FFN_TILE
    f = w_down.shape[0]
    assert sum(FFN_CHUNKS) == f
    chunks = tuple((sum(FFN_CHUNKS[:i]), n) for i, n in enumerate(FFN_CHUNKS))
    weights = 2 * (w_up.size + w_down.size)
    up_buf_bytes = 2 * f * (CONV_HALO + tile) * 4
    tok = lambda b, s: (b, s, 0)
    return pl.pallas_call(
        functools.partial(_ffn_kernel, tile=tile, f=f, chunks=chunks),
        out_shape=jax.ShapeDtypeStruct(x.shape, x.dtype),
        grid=(bsz, seq // tile),
        in_specs=[
            pl.BlockSpec((1, tile, d), tok),
            pl.BlockSpec((1, N_MOD, d), lambda b, s: (b, 0, 0)),
            _const_spec((1, d)), _const_spec((1, d)),
            _const_spec(w_up.shape), _const_spec(conv_w.shape), _const_spec((1, 2 * f)),
            _const_spec(w_down.shape),
        ],
        out_specs=pl.BlockSpec((1, tile, d), tok),
        scratch_shapes=[pltpu.VMEM((2 * f // V7X_LANES, CONV_HALO + tile, V7X_LANES), jnp.float32)],
        compiler_params=pltpu.CompilerParams(
            dimension_semantics=("arbitrary", "arbitrary"),
            vmem_limit_bytes=_vmem_limit(weights + up_buf_bytes, 2 * tile * d * 4, 4 * tile * d * 4)),
        name="convffn",
    )(x, mod, g_pre.reshape(1, d), g_post.reshape(1, d), w_up, conv_w, conv_b.reshape(1, 2 * f), w_down)


def kernel(x, c, g_pre_mix, g_post_mix, g_pre_ffn, g_post_ffn, w_ada, b_ada, w_in, w_pool, pool_scale, conv_w,
           conv_b, w_bout, w_o, w_up, ffn_conv_w, ffn_conv_b, w_down):
    bsz, _, d = x.shape
    for l in range(w_ada.shape[0]):
        mod, w_fold = _prep(c, w_ada[l], b_ada[l], w_in[l], w_pool[l], pool_scale[l])
        mod = mod.reshape(bsz, N_MOD, d)
        x, w_up_bf16, w_down_bf16 = _mixer(x, mod, g_pre_mix[l], g_post_mix[l], w_fold, w_in[l],
                                           conv_w[l], conv_b[l], w_bout[l], w_o[l], w_up[l], w_down[l])
        x = _ffn(x, mod, g_pre_ffn[l], g_post_ffn[l], w_up_bf16, ffn_conv_w[l], ffn_conv_b[l], w_down_bf16)
    return x
```

```python
import functools

import jax
import jax.numpy as jnp
from jax import lax
from jax.experimental import pallas as pl
from jax.experimental.pallas import tpu as pltpu

POOL_WINDOWS = (2, 4, 8, 16)
N_MOD = 6
EPS = 1e-6
GELU_K = 0.7978845608028654
GELU_A = 0.044715

V7X_LANES = 128
V7X_BF16_SUBLANES = 16
V7X_VMEM_BYTES = 64 * 1024 * 1024

POOL_HALO = 16
CONV_HALO = 8

MIXER_TILE = 512
FFN_TILE = 512
PREP_STEPS = 8
PREP_VMEM_CLAIM = 48 * 1024 * 1024
WEIGHT_LOAD_COLS = 512
ROW_SPLIT = 2
FFN_CHUNKS = (1536, 1280)


def _vmem_limit(resident_bytes, streamed_bytes, temp_bytes):
    need = resident_bytes + 2 * streamed_bytes + temp_bytes
    assert need < V7X_VMEM_BYTES, need
    return int(need)


def _const_spec(shape):
    zeros = (0,) * len(shape)
    return pl.BlockSpec(shape, lambda *_: zeros, pipeline_mode=pl.Buffered(1))


def _rmsnorm(xf, gain_row):
    return xf * lax.rsqrt(jnp.mean(xf * xf, axis=-1, keepdims=True) + EPS) * gain_row


def _put_slabs(buf, halo, val, first):
    for i in range(val.shape[1] // V7X_LANES):
        buf[first + i, halo:, :] = val[:, i * V7X_LANES:(i + 1) * V7X_LANES]


def _rows_back(buf, halo, tile, slabs, k):
    return jnp.concatenate([buf[j, halo - k:halo - k + tile, :] for j in slabs], axis=1)


def _keep_history(buf, halo, tile, slabs):
    for j in slabs:
        buf[j, :halo, :] = buf[j, tile:, :]


def _zero_history(buf, halo):
    buf[:, :halo, :] = jnp.zeros((buf.shape[0], halo, V7X_LANES), jnp.float32)


def _dot(a, b):
    return jnp.dot(a, b, preferred_element_type=jnp.float32)


def _row_blocks(rows, steps):
    for n in range(steps, 0, -1):
        if rows % n == 0 and (rows // n) % V7X_BF16_SUBLANES == 0:
            return n, rows // n
    raise ValueError((rows, steps))


def _mixer_kernel(x_ref, gpre_ref, gpost_ref, convw_ref, convb_ref, c_ref, bada_ref, wpool_ref, pscale_ref,
                  wup_f32, wdown_f32, wada_hbm, win_hbm, wbout_hbm, wo_hbm,
                  o_ref, mod_ref, wup_bf16, wdown_bf16,
                  pool_buf, conv_buf, wfold_ref, win_ref, wbout_ref, wo_ref, stage, stage_sem, *, tile, d):
    b, s = pl.program_id(0), pl.program_id(1)
    n_slabs = d // V7X_LANES

    @pl.when((b == 0) & (s == 0))
    def _():
        cols = stage.shape[2]
        gw = wpool_ref.shape[1]
        c_bf16 = c_ref[...].astype(jnp.bfloat16)

        def ada_cols(c0):
            def use(chunk):
                mod_ref[:, c0:c0 + cols] = _dot(c_bf16, chunk.astype(jnp.bfloat16)) + bada_ref[:, c0:c0 + cols]
            return use

        def fold_cols(c0):
            def use(chunk):
                for g in range(c0 // gw, (c0 + cols) // gw):
                    gcols = slice(g * gw, (g + 1) * gw)
                    wp = wpool_ref[g] * (0.5 * pscale_ref[:, gcols])
                    wfold = jnp.dot(chunk[:, g * gw - c0:(g + 1) * gw - c0], wp, precision=lax.Precision.HIGHEST,
                                    preferred_element_type=jnp.float32)
                    wfold_ref[:, gcols] = wfold.astype(jnp.bfloat16)
            return use

        def cast_into(dst):
            def use(chunk):
                dst[...] = chunk.astype(jnp.bfloat16)
            return use

        moves = [(wada_hbm.at[:, pl.ds(c0, cols)], ada_cols(c0)) for c0 in range(0, wada_hbm.shape[1], cols)]
        moves += [(win_hbm.at[:, pl.ds(c0, cols)], fold_cols(c0)) for c0 in range(0, d, cols)]
        moves += [(win_hbm.at[:, pl.ds(c0, cols)], cast_into(win_ref.at[:, pl.ds(c0 - d, cols)]))
                  for c0 in range(d, win_hbm.shape[1], cols)]
        for src, dst in ((wbout_hbm, wbout_ref), (wo_hbm, wo_ref)):
            moves += [(src.at[:, pl.ds(c0, cols)], cast_into(dst.at[:, pl.ds(c0, cols)])) for c0 in range(0, d, cols)]

        def load(i):
            return pltpu.make_async_copy(moves[i][0], stage.at[i % 2], stage_sem.at[i % 2])

        load(0).start()
        for i in range(len(moves)):
            if i + 1 < len(moves):
                load(i + 1).start()
            load(i).wait()
            moves[i][1](stage[i % 2])

    wup_bf16[...] = wup_f32[...].astype(jnp.bfloat16)
    wdown_bf16[...] = wdown_f32[...].astype(jnp.bfloat16)

    @pl.when(s == 0)
    def _():
        _zero_history(pool_buf, POOL_HALO)
        _zero_history(conv_buf, CONV_HALO)

    x = x_ref[0]
    sh1, sc1, gt1 = (mod_ref[pl.ds(b, 1), i * d:(i + 1) * d] for i in range(3))
    halves = [slice(i * tile // ROW_SPLIT, (i + 1) * tile // ROW_SPLIT) for i in range(ROW_SPLIT)]
    pre_gain = gpre_ref[...] * (1.0 + sc1)
    hb_parts = [(_rmsnorm(x[rows], pre_gain) + sh1).astype(jnp.bfloat16) for rows in halves]
    hb = jnp.concatenate(hb_parts, axis=0)

    def proj(i):
        return _dot(hb, win_ref[:, (i - 1) * d:i * d])

    u = jnp.concatenate([_dot(part, wfold_ref[...]) for part in hb_parts], axis=0)
    _put_slabs(pool_buf, POOL_HALO, u, 0)
    v = proj(3) * proj(1)
    _put_slabs(conv_buf, CONV_HALO, v, 0)

    gw = d // len(POOL_WINDOWS)
    t1 = (s * tile + lax.broadcasted_iota(jnp.int32, (tile, V7X_LANES), 0) + 1).astype(jnp.float32)
    half_ya = []
    for g, w in enumerate(POOL_WINDOWS):
        slabs = range(g * gw // V7X_LANES, (g + 1) * gw // V7X_LANES)
        ug = u[:, g * gw:(g + 1) * gw]
        acc = ug
        for k in range(1, w):
            acc = acc + _rows_back(pool_buf, POOL_HALO, tile, slabs, k)
        inv = 1.0 / jnp.minimum(t1, float(w))
        inv = jnp.concatenate([inv] * len(slabs), axis=1)
        half_ya.append(acc * inv - ug)
    _keep_history(pool_buf, POOL_HALO, tile, range(n_slabs))
    half_ya = jnp.concatenate(half_ya, axis=1)

    ub = proj(2)

    v1 = _rows_back(conv_buf, CONV_HALO, tile, range(n_slabs), 1)
    v2 = _rows_back(conv_buf, CONV_HALO, tile, range(n_slabs), 2)
    _keep_history(conv_buf, CONV_HALO, tile, range(n_slabs))
    cw, cb = 0.5 * convw_ref[...], 0.5 * convb_ref[...]
    half_conv = cb + cw[0:1] * v2 + cw[1:2] * v1 + cw[2:3] * v
    half_ubc = (ub * half_conv).astype(jnp.bfloat16)

    za = proj(4)
    zb = proj(5)
    gated_a = half_ya + half_ya * jnp.tanh(0.5 * za)
    half_yb = _dot(half_ubc, wbout_ref[...])
    merged = (gated_a + (half_yb + half_yb * jnp.tanh(0.5 * zb))).astype(jnp.bfloat16)
    post_gain = gpost_ref[...] * gt1
    for rows in halves:
        o_ref[0, rows, :] = x[rows] + _rmsnorm(_dot(merged[rows], wo_ref[...]), post_gain)


def _mixer(x, c, w_ada, b_ada, g_pre, g_post, w_in, w_pool, pool_scale, conv_w, conv_b, w_bout, w_o, w_up, w_down):
    bsz, seq, d = x.shape
    tile = MIXER_TILE
    n_seq = seq // tile
    n_mod = w_ada.shape[1]
    d_rest = w_in.shape[1] - d
    gw = w_pool.shape[1]
    assert d_rest % WEIGHT_LOAD_COLS == 0 and d % WEIGHT_LOAD_COLS == 0 and n_mod % WEIGHT_LOAD_COLS == 0
    assert WEIGHT_LOAD_COLS % gw == 0 and w_ada.shape[0] == d
    weights = (2 * (d * d + d * d_rest + w_bout.size + w_o.size) + 2 * d * WEIGHT_LOAD_COLS * 4
               + w_pool.size * 4 + 2 * bsz * n_mod * 4)
    tok = lambda b, s: (b, s, 0)
    hbm = pl.BlockSpec(memory_space=pl.ANY)

    def side_spec(w):
        n_blocks, rows = _row_blocks(w.shape[0], bsz * n_seq)
        return pl.BlockSpec((rows, w.shape[1]), lambda b, s: (jnp.minimum(b * n_seq + s, n_blocks - 1), 0)), rows

    (up_spec, up_rows), (down_spec, down_rows) = side_spec(w_up), side_spec(w_down)
    side_bytes = (up_rows * w_up.shape[1] + down_rows * w_down.shape[1]) * (4 + 2)
    hist_bytes = d * (POOL_HALO + CONV_HALO + 2 * tile) * 4
    return pl.pallas_call(
        functools.partial(_mixer_kernel, tile=tile, d=d),
        out_shape=(jax.ShapeDtypeStruct(x.shape, x.dtype), jax.ShapeDtypeStruct((bsz, n_mod), jnp.float32),
                   jax.ShapeDtypeStruct(w_up.shape, jnp.bfloat16), jax.ShapeDtypeStruct(w_down.shape, jnp.bfloat16)),
        grid=(bsz, n_seq),
        in_specs=[
            pl.BlockSpec((1, tile, d), tok),
            _const_spec((1, d)), _const_spec((1, d)),
            _const_spec(conv_w.shape), _const_spec((1, d)),
            _const_spec(c.shape), _const_spec((1, n_mod)), _const_spec(w_pool.shape), _const_spec((1, d)),
            up_spec, down_spec,
            hbm, hbm, hbm, hbm,
        ],
        out_specs=(pl.BlockSpec((1, tile, d), tok), pl.BlockSpec((bsz, n_mod), lambda b, s: (0, 0)),
                   up_spec, down_spec),
        scratch_shapes=[pltpu.VMEM((d // V7X_LANES, POOL_HALO + tile, V7X_LANES), jnp.float32),
                        pltpu.VMEM((d // V7X_LANES, CONV_HALO + tile, V7X_LANES), jnp.float32),
                        pltpu.VMEM((d, d), jnp.bfloat16),
                        pltpu.VMEM((d, d_rest), jnp.bfloat16), pltpu.VMEM(w_bout.shape, jnp.bfloat16),
                        pltpu.VMEM(w_o.shape, jnp.bfloat16),
                        pltpu.VMEM((2, d, WEIGHT_LOAD_COLS), jnp.float32), pltpu.SemaphoreType.DMA((2,))],
        compiler_params=pltpu.CompilerParams(
            dimension_semantics=("arbitrary", "arbitrary"),
            vmem_limit_bytes=_vmem_limit(weights + hist_bytes, 2 * tile * d * 4 + side_bytes, 8 * tile * d * 4)),
        name="mixer",
    )(x, g_pre.reshape(1, d), g_post.reshape(1, d), conv_w, conv_b.reshape(1, d),
      c, b_ada.reshape(1, n_mod), w_pool, pool_scale.reshape(1, d), w_up, w_down, w_ada, w_in, w_bout, w_o)


def _ffn_kernel(x_ref, mod_ref, gpre_ref, gpost_ref, wup_ref, cw_ref, cb_ref, wdown_ref, o_ref, up_buf,
                *, tile, f, chunks):
    s = pl.program_id(1)

    @pl.when(s == 0)
    def _():
        _zero_history(up_buf, CONV_HALO)

    x = x_ref[0]
    mod = mod_ref[0]
    sh2, sc2, gt2 = mod[3:4], mod[4:5], mod[5:6]
    hb = (_rmsnorm(x, gpre_ref[...] * (1.0 + sc2)) + sh2).astype(jnp.bfloat16)

    def conv_cols(lo, n, scale=None):
        cols = slice(lo, lo + n)
        up = _dot(hb, wup_ref[:, cols])
        slabs = range(lo // V7X_LANES, (lo + n) // V7X_LANES)
        _put_slabs(up_buf, CONV_HALO, up, slabs[0])
        u1 = _rows_back(up_buf, CONV_HALO, tile, slabs, 1)
        u2 = _rows_back(up_buf, CONV_HALO, tile, slabs, 2)
        _keep_history(up_buf, CONV_HALO, tile, slabs)
        cw, cb = cw_ref[:, cols], cb_ref[:, cols]
        if scale is not None:
            cw, cb = cw * scale, cb * scale
        return cb + cw[0:1] * u2 + cw[1:2] * u1 + cw[2:3] * up

    ff = None
    for lo, n in chunks:
        gate = conv_cols(lo, n)
        half_val = conv_cols(f + lo, n, scale=0.5)
        p = gate * half_val
        t = jnp.tanh(gate * (GELU_K + (GELU_K * GELU_A) * (gate * gate)))
        act = (p + p * t).astype(jnp.bfloat16)
        part = _dot(act, wdown_ref[lo:lo + n, :])
        ff = part if ff is None else ff + part
    o_ref[0] = x + _rmsnorm(ff, gpost_ref[...] * gt2)


def _ffn(x, mod, g_pre, g_post, w_up, conv_w, conv_b, w_down):
    bsz, seq, d = x.shape
    tile = FFN_TILE
    f = w_down.shape[0]
    assert sum(FFN_CHUNKS) == f
    chunks = tuple((sum(FFN_CHUNKS[:i]), n) for i, n in enumerate(FFN_CHUNKS))
    weights = 2 * (w_up.size + w_down.size)
    up_buf_bytes = 2 * f * (CONV_HALO + tile) * 4
    tok = lambda b, s: (b, s, 0)
    return pl.pallas_call(
        functools.partial(_ffn_kernel, tile=tile, f=f, chunks=chunks),
        out_shape=jax.ShapeDtypeStruct(x.shape, x.dtype),
        grid=(bsz, seq // tile),
        in_specs=[
            pl.BlockSpec((1, tile, d), tok),
            pl.BlockSpec((1, N_MOD, d), lambda b, s: (b, 0, 0)),
            _const_spec((1, d)), _const_spec((1, d)),
            _const_spec(w_up.shape), _const_spec(conv_w.shape), _const_spec((1, 2 * f)),
            _const_spec(w_down.shape),
        ],
        out_specs=pl.BlockSpec((1, tile, d), tok),
        scratch_shapes=[pltpu.VMEM((2 * f // V7X_LANES, CONV_HALO + tile, V7X_LANES), jnp.float32)],
        compiler_params=pltpu.CompilerParams(
            dimension_semantics=("arbitrary", "arbitrary"),
            vmem_limit_bytes=_vmem_limit(weights + up_buf_bytes, 2 * tile * d * 4, 4 * tile * d * 4)),
        name="convffn",
    )(x, mod, g_pre.reshape(1, d), g_post.reshape(1, d), w_up, conv_w, conv_b.reshape(1, 2 * f), w_down)


def kernel(x, c, g_pre_mix, g_post_mix, g_pre_ffn, g_post_ffn, w_ada, b_ada, w_in, w_pool, pool_scale, conv_w,
           conv_b, w_bout, w_o, w_up, ffn_conv_w, ffn_conv_b, w_down):
    bsz, _, d = x.shape
    for l in range(w_ada.shape[0]):
        x, mod, w_up_bf16, w_down_bf16 = _mixer(x, c, w_ada[l], b_ada[l], g_pre_mix[l], g_post_mix[l], w_in[l],
                                                w_pool[l], pool_scale[l], conv_w[l], conv_b[l], w_bout[l], w_o[l],
                                                w_up[l], w_down[l])
        mod = mod.reshape(bsz, N_MOD, d)
        x = _ffn(x, mod, g_pre_ffn[l], g_post_ffn[l], w_up_bf16, ffn_conv_w[l], ffn_conv_b[l], w_down_bf16)
    return x
```

```python
import functools

import jax
import jax.numpy as jnp
from jax import lax
from jax.experimental import pallas as pl
from jax.experimental.pallas import tpu as pltpu

POOL_WINDOWS = (2, 4, 8, 16)
N_MOD = 6
EPS = 1e-6
GELU_K = 0.7978845608028654
GELU_A = 0.044715

V7X_LANES = 128
V7X_BF16_SUBLANES = 16
V7X_VMEM_BYTES = 64 * 1024 * 1024

POOL_HALO = 16
CONV_HALO = 8

MIXER_TILE = 512
FFN_TILE = 512
WEIGHT_LOAD_COLS = 512
WEIGHT_LOAD_SLOTS = 3
ROW_SPLIT = 2
FFN_CHUNKS = (1536, 1280)


def _vmem_limit(resident_bytes, streamed_bytes, temp_bytes):
    need = resident_bytes + 2 * streamed_bytes + temp_bytes
    assert need < V7X_VMEM_BYTES, need
    return int(need)


def _const_spec(shape):
    zeros = (0,) * len(shape)
    return pl.BlockSpec(shape, lambda *_: zeros, pipeline_mode=pl.Buffered(1))


def _rmsnorm(xf, gain_row):
    return xf * lax.rsqrt(jnp.mean(xf * xf, axis=-1, keepdims=True) + EPS) * gain_row


def _put_slabs(buf, halo, val, first):
    for i in range(val.shape[1] // V7X_LANES):
        buf[first + i, halo:, :] = val[:, i * V7X_LANES:(i + 1) * V7X_LANES]


def _rows_back(buf, halo, tile, slabs, k):
    return jnp.concatenate([buf[j, halo - k:halo - k + tile, :] for j in slabs], axis=1)


def _keep_history(buf, halo, tile, slabs):
    for j in slabs:
        buf[j, :halo, :] = buf[j, tile:, :]


def _zero_history(buf, halo):
    buf[:, :halo, :] = jnp.zeros((buf.shape[0], halo, V7X_LANES), jnp.float32)


def _dot(a, b):
    return jnp.dot(a, b, preferred_element_type=jnp.float32)


def _row_blocks(rows, steps):
    for n in range(steps, 0, -1):
        if rows % n == 0 and (rows // n) % V7X_BF16_SUBLANES == 0:
            return n, rows // n
    raise ValueError((rows, steps))


def _mixer_kernel(x_ref, gpre_ref, gpost_ref, convw_ref, convb_ref, c_ref, bada_ref, wpool_ref, pscale_ref,
                  wup_f32, wdown_f32, wada_hbm, win_hbm, wbout_hbm, wo_hbm,
                  o_ref, mod_ref, wup_bf16, wdown_bf16,
                  pool_buf, conv_buf, wfold_ref, win_ref, wbout_ref, wo_ref, stage, stage_sem, *, tile, d):
    b, s = pl.program_id(0), pl.program_id(1)
    n_slabs = d // V7X_LANES

    @pl.when((b == 0) & (s == 0))
    def _():
        cols = stage.shape[2]
        gw = wpool_ref.shape[1]
        c_bf16 = c_ref[...].astype(jnp.bfloat16)

        def ada_cols(c0):
            def use(chunk):
                mod_ref[:, c0:c0 + cols] = _dot(c_bf16, chunk.astype(jnp.bfloat16)) + bada_ref[:, c0:c0 + cols]
            return use

        def fold_cols(c0):
            def use(chunk):
                for g in range(c0 // gw, (c0 + cols) // gw):
                    gcols = slice(g * gw, (g + 1) * gw)
                    wp = wpool_ref[g] * (0.5 * pscale_ref[:, gcols])
                    wfold = jnp.dot(chunk[:, g * gw - c0:(g + 1) * gw - c0], wp, precision=lax.Precision.HIGHEST,
                                    preferred_element_type=jnp.float32)
                    wfold_ref[:, gcols] = wfold.astype(jnp.bfloat16)
            return use

        def cast_into(dst):
            def use(chunk):
                dst[...] = chunk.astype(jnp.bfloat16)
            return use

        moves = [(wada_hbm.at[:, pl.ds(c0, cols)], ada_cols(c0)) for c0 in range(0, wada_hbm.shape[1], cols)]
        moves += [(win_hbm.at[:, pl.ds(c0, cols)], fold_cols(c0)) for c0 in range(0, d, cols)]
        moves += [(win_hbm.at[:, pl.ds(c0, cols)], cast_into(win_ref.at[:, pl.ds(c0 - d, cols)]))
                  for c0 in range(d, win_hbm.shape[1], cols)]
        for src, dst in ((wbout_hbm, wbout_ref), (wo_hbm, wo_ref)):
            moves += [(src.at[:, pl.ds(c0, cols)], cast_into(dst.at[:, pl.ds(c0, cols)])) for c0 in range(0, d, cols)]

        n_slots = stage.shape[0]

        def load(i):
            return pltpu.make_async_copy(moves[i][0], stage.at[i % n_slots], stage_sem.at[i % n_slots])

        for i in range(n_slots - 1):
            load(i).start()
        for i in range(len(moves)):
            if i + n_slots - 1 < len(moves):
                load(i + n_slots - 1).start()
            load(i).wait()
            moves[i][1](stage[i % n_slots])

    wup_bf16[...] = wup_f32[...].astype(jnp.bfloat16)
    wdown_bf16[...] = wdown_f32[...].astype(jnp.bfloat16)

    @pl.when(s == 0)
    def _():
        _zero_history(pool_buf, POOL_HALO)
        _zero_history(conv_buf, CONV_HALO)

    x = x_ref[0]
    sh1, sc1, gt1 = (mod_ref[pl.ds(b, 1), i * d:(i + 1) * d] for i in range(3))
    halves = [slice(i * tile // ROW_SPLIT, (i + 1) * tile // ROW_SPLIT) for i in range(ROW_SPLIT)]
    pre_gain = gpre_ref[...] * (1.0 + sc1)
    hb_parts = [(_rmsnorm(x[rows], pre_gain) + sh1).astype(jnp.bfloat16) for rows in halves]
    hb = jnp.concatenate(hb_parts, axis=0)

    def proj(i):
        return _dot(hb, win_ref[:, (i - 1) * d:i * d])

    u = jnp.concatenate([_dot(part, wfold_ref[...]) for part in hb_parts], axis=0)
    _put_slabs(pool_buf, POOL_HALO, u, 0)
    v = proj(3) * proj(1)
    _put_slabs(conv_buf, CONV_HALO, v, 0)

    gw = d // len(POOL_WINDOWS)
    t1 = (s * tile + lax.broadcasted_iota(jnp.int32, (tile, V7X_LANES), 0) + 1).astype(jnp.float32)
    half_ya = []
    for g, w in enumerate(POOL_WINDOWS):
        slabs = range(g * gw // V7X_LANES, (g + 1) * gw // V7X_LANES)
        ug = u[:, g * gw:(g + 1) * gw]
        acc = ug
        for k in range(1, w):
            acc = acc + _rows_back(pool_buf, POOL_HALO, tile, slabs, k)
        inv = 1.0 / jnp.minimum(t1, float(w))
        inv = jnp.concatenate([inv] * len(slabs), axis=1)
        half_ya.append(acc * inv - ug)
    _keep_history(pool_buf, POOL_HALO, tile, range(n_slabs))
    half_ya = jnp.concatenate(half_ya, axis=1)

    ub = proj(2)

    v1 = _rows_back(conv_buf, CONV_HALO, tile, range(n_slabs), 1)
    v2 = _rows_back(conv_buf, CONV_HALO, tile, range(n_slabs), 2)
    _keep_history(conv_buf, CONV_HALO, tile, range(n_slabs))
    cw, cb = 0.5 * convw_ref[...], 0.5 * convb_ref[...]
    half_conv = cb + cw[0:1] * v2 + cw[1:2] * v1 + cw[2:3] * v
    half_ubc = (ub * half_conv).astype(jnp.bfloat16)

    za = proj(4)
    zb = proj(5)
    gated_a = half_ya + half_ya * jnp.tanh(0.5 * za)
    half_yb = _dot(half_ubc, wbout_ref[...])
    merged = (gated_a + (half_yb + half_yb * jnp.tanh(0.5 * zb))).astype(jnp.bfloat16)
    post_gain = gpost_ref[...] * gt1
    for rows in halves:
        o_ref[0, rows, :] = x[rows] + _rmsnorm(_dot(merged[rows], wo_ref[...]), post_gain)


def _mixer(x, c, w_ada, b_ada, g_pre, g_post, w_in, w_pool, pool_scale, conv_w, conv_b, w_bout, w_o, w_up, w_down):
    bsz, seq, d = x.shape
    tile = MIXER_TILE
    n_seq = seq // tile
    n_mod = w_ada.shape[1]
    d_rest = w_in.shape[1] - d
    gw = w_pool.shape[1]
    assert d_rest % WEIGHT_LOAD_COLS == 0 and d % WEIGHT_LOAD_COLS == 0 and n_mod % WEIGHT_LOAD_COLS == 0
    assert WEIGHT_LOAD_COLS % gw == 0 and w_ada.shape[0] == d
    weights = (2 * (d * d + d * d_rest + w_bout.size + w_o.size) + WEIGHT_LOAD_SLOTS * d * WEIGHT_LOAD_COLS * 4
               + w_pool.size * 4 + 2 * bsz * n_mod * 4)
    tok = lambda b, s: (b, s, 0)
    hbm = pl.BlockSpec(memory_space=pl.ANY)

    def side_spec(w):
        n_blocks, rows = _row_blocks(w.shape[0], bsz * n_seq)
        return pl.BlockSpec((rows, w.shape[1]), lambda b, s: (jnp.minimum(b * n_seq + s, n_blocks - 1), 0)), rows

    (up_spec, up_rows), (down_spec, down_rows) = side_spec(w_up), side_spec(w_down)
    side_bytes = (up_rows * w_up.shape[1] + down_rows * w_down.shape[1]) * (4 + 2)
    hist_bytes = d * (POOL_HALO + CONV_HALO + 2 * tile) * 4
    return pl.pallas_call(
        functools.partial(_mixer_kernel, tile=tile, d=d),
        out_shape=(jax.ShapeDtypeStruct(x.shape, x.dtype), jax.ShapeDtypeStruct((bsz, n_mod), jnp.float32),
                   jax.ShapeDtypeStruct(w_up.shape, jnp.bfloat16), jax.ShapeDtypeStruct(w_down.shape, jnp.bfloat16)),
        grid=(bsz, n_seq),
        in_specs=[
            pl.BlockSpec((1, tile, d), tok),
            _const_spec((1, d)), _const_spec((1, d)),
            _const_spec(conv_w.shape), _const_spec((1, d)),
            _const_spec(c.shape), _const_spec((1, n_mod)), _const_spec(w_pool.shape), _const_spec((1, d)),
            up_spec, down_spec,
            hbm, hbm, hbm, hbm,
        ],
        out_specs=(pl.BlockSpec((1, tile, d), tok), pl.BlockSpec((bsz, n_mod), lambda b, s: (0, 0)),
                   up_spec, down_spec),
        scratch_shapes=[pltpu.VMEM((d // V7X_LANES, POOL_HALO + tile, V7X_LANES), jnp.float32),
                        pltpu.VMEM((d // V7X_LANES, CONV_HALO + tile, V7X_LANES), jnp.float32),
                        pltpu.VMEM((d, d), jnp.bfloat16),
                        pltpu.VMEM((d, d_rest), jnp.bfloat16), pltpu.VMEM(w_bout.shape, jnp.bfloat16),
                        pltpu.VMEM(w_o.shape, jnp.bfloat16),
                        pltpu.VMEM((WEIGHT_LOAD_SLOTS, d, WEIGHT_LOAD_COLS), jnp.float32),
                        pltpu.SemaphoreType.DMA((WEIGHT_LOAD_SLOTS,))],
        compiler_params=pltpu.CompilerParams(
            dimension_semantics=("arbitrary", "arbitrary"),
            vmem_limit_bytes=_vmem_limit(weights + hist_bytes, 2 * tile * d * 4 + side_bytes, 8 * tile * d * 4)),
        name="mixer",
    )(x, g_pre.reshape(1, d), g_post.reshape(1, d), conv_w, conv_b.reshape(1, d),
      c, b_ada.reshape(1, n_mod), w_pool, pool_scale.reshape(1, d), w_up, w_down, w_ada, w_in, w_bout, w_o)


def _ffn_kernel(x_ref, mod_ref, gpre_ref, gpost_ref, wup_ref, cw_ref, cb_ref, wdown_ref, o_ref, up_buf,
                *, tile, f, chunks):
    s = pl.program_id(1)

    @pl.when(s == 0)
    def _():
        _zero_history(up_buf, CONV_HALO)

    x = x_ref[0]
    mod = mod_ref[0]
    sh2, sc2, gt2 = mod[3:4], mod[4:5], mod[5:6]
    hb = (_rmsnorm(x, gpre_ref[...] * (1.0 + sc2)) + sh2).astype(jnp.bfloat16)

    def conv_cols(lo, n, scale=None):
        cols = slice(lo, lo + n)
        up = _dot(hb, wup_ref[:, cols])
        slabs = range(lo // V7X_LANES, (lo + n) // V7X_LANES)
        _put_slabs(up_buf, CONV_HALO, up, slabs[0])
        u1 = _rows_back(up_buf, CONV_HALO, tile, slabs, 1)
        u2 = _rows_back(up_buf, CONV_HALO, tile, slabs, 2)
        _keep_history(up_buf, CONV_HALO, tile, slabs)
        cw, cb = cw_ref[:, cols], cb_ref[:, cols]
        if scale is not None:
            cw, cb = cw * scale, cb * scale
        return cb + cw[0:1] * u2 + cw[1:2] * u1 + cw[2:3] * up

    ff = None
    for lo, n in chunks:
        gate = conv_cols(lo, n)
        half_val = conv_cols(f + lo, n, scale=0.5)
        p = gate * half_val
        t = jnp.tanh(gate * (GELU_K + (GELU_K * GELU_A) * (gate * gate)))
        act = (p + p * t).astype(jnp.bfloat16)
        part = _dot(act, wdown_ref[lo:lo + n, :])
        ff = part if ff is None else ff + part
    o_ref[0] = x + _rmsnorm(ff, gpost_ref[...] * gt2)


def _ffn(x, mod, g_pre, g_post, w_up, conv_w, conv_b, w_down):
    bsz, seq, d = x.shape
    tile = FFN_TILE
    f = w_down.shape[0]
    assert sum(FFN_CHUNKS) == f
    chunks = tuple((sum(FFN_CHUNKS[:i]), n) for i, n in enumerate(FFN_CHUNKS))
    weights = 2 * (w_up.size + w_down.size)
    up_buf_bytes = 2 * f * (CONV_HALO + tile) * 4
    tok = lambda b, s: (b, s, 0)
    return pl.pallas_call(
        functools.partial(_ffn_kernel, tile=tile, f=f, chunks=chunks),
        out_shape=jax.ShapeDtypeStruct(x.shape, x.dtype),
        grid=(bsz, seq // tile),
        in_specs=[
            pl.BlockSpec((1, tile, d), tok),
            pl.BlockSpec((1, N_MOD, d), lambda b, s: (b, 0, 0)),
            _const_spec((1, d)), _const_spec((1, d)),
            _const_spec(w_up.shape), _const_spec(conv_w.shape), _const_spec((1, 2 * f)),
            _const_spec(w_down.shape),
        ],
        out_specs=pl.BlockSpec((1, tile, d), tok),
        scratch_shapes=[pltpu.VMEM((2 * f // V7X_LANES, CONV_HALO + tile, V7X_LANES), jnp.float32)],
        compiler_params=pltpu.CompilerParams(
            dimension_semantics=("arbitrary", "arbitrary"),
            vmem_limit_bytes=_vmem_limit(weights + up_buf_bytes, 2 * tile * d * 4, 4 * tile * d * 4)),
        name="convffn",
    )(x, mod, g_pre.reshape(1, d), g_post.reshape(1, d), w_up, conv_w, conv_b.reshape(1, 2 * f), w_down)


def kernel(x, c, g_pre_mix, g_post_mix, g_pre_ffn, g_post_ffn, w_ada, b_ada, w_in, w_pool, pool_scale, conv_w,
           conv_b, w_bout, w_o, w_up, ffn_conv_w, ffn_conv_b, w_down):
    bsz, _, d = x.shape
    for l in range(w_ada.shape[0]):
        x, mod, w_up_bf16, w_down_bf16 = _mixer(x, c, w_ada[l], b_ada[l], g_pre_mix[l], g_post_mix[l], w_in[l],
                                                w_pool[l], pool_scale[l], conv_w[l], conv_b[l], w_bout[l], w_o[l],
                                                w_up[l], w_down[l])
        mod = mod.reshape(bsz, N_MOD, d)
        x = _ffn(x, mod, g_pre_ffn[l], g_post_ffn[l], w_up_bf16, ffn_conv_w[l], ffn_conv_b[l], w_down_bf16)
    return x
```

```python
import functools

import jax
import jax.numpy as jnp
from jax import lax
from jax.experimental import pallas as pl
from jax.experimental.pallas import tpu as pltpu

POOL_WINDOWS = (2, 4, 8, 16)
N_MOD = 6
EPS = 1e-6
GELU_K = 0.7978845608028654
GELU_A = 0.044715

V7X_LANES = 128
V7X_BF16_SUBLANES = 16
V7X_VMEM_BYTES = 64 * 1024 * 1024
V7X_SCOPED_VMEM_MAX_BYTES = 60000 * 1024

POOL_HALO = 16
CONV_HALO = 8

MIXER_TILE = 512
FFN_TILE = 512
WEIGHT_LOAD_COLS = 512
WEIGHT_LOAD_SLOTS = 3
ROW_SPLIT = 2
FFN_CHUNKS = (1536, 1280)


def _vmem_limit(resident_bytes, streamed_bytes, temp_bytes):
    need = resident_bytes + 2 * streamed_bytes + temp_bytes
    assert need < V7X_VMEM_BYTES, need
    return int(need)


def _const_spec(shape):
    zeros = (0,) * len(shape)
    return pl.BlockSpec(shape, lambda *_: zeros, pipeline_mode=pl.Buffered(1))


def _layer_spec(stacked, layer):
    zeros = (0,) * (stacked.ndim - 1)
    return pl.BlockSpec((1,) + stacked.shape[1:], lambda *_: (layer,) + zeros, pipeline_mode=pl.Buffered(1))


def _rmsnorm(xf, gain_row):
    return xf * lax.rsqrt(jnp.mean(xf * xf, axis=-1, keepdims=True) + EPS) * gain_row


def _put_slabs(buf, halo, val, first):
    for i in range(val.shape[1] // V7X_LANES):
        buf[first + i, halo:, :] = val[:, i * V7X_LANES:(i + 1) * V7X_LANES]


def _rows_back(buf, halo, tile, slabs, k):
    return jnp.concatenate([buf[j, halo - k:halo - k + tile, :] for j in slabs], axis=1)


def _keep_history(buf, halo, tile, slabs):
    for j in slabs:
        buf[j, :halo, :] = buf[j, tile:, :]


def _zero_history(buf, halo):
    buf[:, :halo, :] = jnp.zeros((buf.shape[0], halo, V7X_LANES), jnp.float32)


def _dot(a, b):
    return jnp.dot(a, b, preferred_element_type=jnp.float32)


def _row_blocks(rows, steps):
    for n in range(steps, 0, -1):
        if rows % n == 0 and (rows // n) % V7X_BF16_SUBLANES == 0:
            return n, rows // n
    raise ValueError((rows, steps))


def _mixer_kernel(x_ref, gpre_ref, gpost_ref, convw_ref, convb_ref, c_ref, bada_ref, wpool_ref, pscale_ref,
                  wup_f32, wdown_f32, wada_hbm, win_hbm, wbout_hbm, wo_hbm,
                  o_ref, mod_ref, wup_bf16, wdown_bf16,
                  pool_buf, conv_buf, wfold_ref, win_ref, wbout_ref, wo_ref, stage, stage_sem, *, tile, d):
    b, s = pl.program_id(0), pl.program_id(1)
    n_slabs = d // V7X_LANES

    @pl.when((b == 0) & (s == 0))
    def _():
        cols = stage.shape[2]
        gw = wpool_ref.shape[1]
        c_bf16 = c_ref[...].astype(jnp.bfloat16)

        def ada_cols(c0):
            def use(chunk):
                mod_ref[:, c0:c0 + cols] = _dot(c_bf16, chunk.astype(jnp.bfloat16)) + bada_ref[:, c0:c0 + cols]
            return use

        def fold_cols(c0):
            def use(chunk):
                for g in range(c0 // gw, (c0 + cols) // gw):
                    gcols = slice(g * gw, (g + 1) * gw)
                    wp = wpool_ref[g] * (0.5 * pscale_ref[:, gcols])
                    wfold = jnp.dot(chunk[:, g * gw - c0:(g + 1) * gw - c0], wp, precision=lax.Precision.HIGHEST,
                                    preferred_element_type=jnp.float32)
                    wfold_ref[:, gcols] = wfold.astype(jnp.bfloat16)
            return use

        def cast_into(dst):
            def use(chunk):
                dst[...] = chunk.astype(jnp.bfloat16)
            return use

        moves = [(wada_hbm.at[:, pl.ds(c0, cols)], ada_cols(c0)) for c0 in range(0, wada_hbm.shape[1], cols)]
        moves += [(win_hbm.at[:, pl.ds(c0, cols)], fold_cols(c0)) for c0 in range(0, d, cols)]
        moves += [(win_hbm.at[:, pl.ds(c0, cols)], cast_into(win_ref.at[:, pl.ds(c0 - d, cols)]))
                  for c0 in range(d, win_hbm.shape[1], cols)]
        for src, dst in ((wbout_hbm, wbout_ref), (wo_hbm, wo_ref)):
            moves += [(src.at[:, pl.ds(c0, cols)], cast_into(dst.at[:, pl.ds(c0, cols)])) for c0 in range(0, d, cols)]

        n_slots = stage.shape[0]

        def load(i):
            return pltpu.make_async_copy(moves[i][0], stage.at[i % n_slots], stage_sem.at[i % n_slots])

        for i in range(n_slots - 1):
            load(i).start()
        for i in range(len(moves)):
            if i + n_slots - 1 < len(moves):
                load(i + n_slots - 1).start()
            load(i).wait()
            moves[i][1](stage[i % n_slots])

    wup_bf16[...] = wup_f32[...].astype(jnp.bfloat16)
    wdown_bf16[...] = wdown_f32[...].astype(jnp.bfloat16)

    @pl.when(s == 0)
    def _():
        _zero_history(pool_buf, POOL_HALO)
        _zero_history(conv_buf, CONV_HALO)

    x = x_ref[0]
    sh1, sc1, gt1 = (mod_ref[pl.ds(b, 1), i * d:(i + 1) * d] for i in range(3))
    halves = [slice(i * tile // ROW_SPLIT, (i + 1) * tile // ROW_SPLIT) for i in range(ROW_SPLIT)]
    pre_gain = gpre_ref[...] * (1.0 + sc1)
    hb_parts = [(_rmsnorm(x[rows], pre_gain) + sh1).astype(jnp.bfloat16) for rows in halves]
    hb = jnp.concatenate(hb_parts, axis=0)

    def proj(i):
        return _dot(hb, win_ref[:, (i - 1) * d:i * d])

    u = jnp.concatenate([_dot(part, wfold_ref[...]) for part in hb_parts], axis=0)
    _put_slabs(pool_buf, POOL_HALO, u, 0)
    v = proj(3) * proj(1)
    _put_slabs(conv_buf, CONV_HALO, v, 0)

    gw = d // len(POOL_WINDOWS)
    t1 = (s * tile + lax.broadcasted_iota(jnp.int32, (tile, V7X_LANES), 0) + 1).astype(jnp.float32)
    half_ya = []
    for g, w in enumerate(POOL_WINDOWS):
        slabs = range(g * gw // V7X_LANES, (g + 1) * gw // V7X_LANES)
        ug = u[:, g * gw:(g + 1) * gw]
        acc = ug
        for k in range(1, w):
            acc = acc + _rows_back(pool_buf, POOL_HALO, tile, slabs, k)
        inv = 1.0 / jnp.minimum(t1, float(w))
        inv = jnp.concatenate([inv] * len(slabs), axis=1)
        half_ya.append(acc * inv - ug)
    _keep_history(pool_buf, POOL_HALO, tile, range(n_slabs))
    half_ya = jnp.concatenate(half_ya, axis=1)

    ub = proj(2)

    v1 = _rows_back(conv_buf, CONV_HALO, tile, range(n_slabs), 1)
    v2 = _rows_back(conv_buf, CONV_HALO, tile, range(n_slabs), 2)
    _keep_history(conv_buf, CONV_HALO, tile, range(n_slabs))
    cw, cb = 0.5 * convw_ref[0], 0.5 * convb_ref[...]
    half_conv = cb + cw[0:1] * v2 + cw[1:2] * v1 + cw[2:3] * v
    half_ubc = (ub * half_conv).astype(jnp.bfloat16)

    za = proj(4)
    zb = proj(5)
    gated_a = half_ya + half_ya * jnp.tanh(0.5 * za)
    half_yb = _dot(half_ubc, wbout_ref[...])
    merged = (gated_a + (half_yb + half_yb * jnp.tanh(0.5 * zb))).astype(jnp.bfloat16)
    post_gain = gpost_ref[...] * gt1
    for rows in halves:
        o_ref[0, rows, :] = x[rows] + _rmsnorm(_dot(merged[rows], wo_ref[...]), post_gain)


def _mixer(x, c, w_ada, b_ada, g_pre, g_post, w_in, w_pool, pool_scale, conv_w, layer, conv_b, w_bout, w_o, w_up,
           w_down):
    bsz, seq, d = x.shape
    tile = MIXER_TILE
    n_seq = seq // tile
    n_mod = w_ada.shape[1]
    d_rest = w_in.shape[1] - d
    gw = w_pool.shape[1]
    assert d_rest % WEIGHT_LOAD_COLS == 0 and d % WEIGHT_LOAD_COLS == 0 and n_mod % WEIGHT_LOAD_COLS == 0
    assert WEIGHT_LOAD_COLS % gw == 0 and w_ada.shape[0] == d
    weights = (2 * (d * d + d * d_rest + w_bout.size + w_o.size) + WEIGHT_LOAD_SLOTS * d * WEIGHT_LOAD_COLS * 4
               + w_pool.size * 4 + 2 * bsz * n_mod * 4)
    tok = lambda b, s: (b, s, 0)
    hbm = pl.BlockSpec(memory_space=pl.ANY)

    def side_spec(w):
        n_blocks, rows = _row_blocks(w.shape[0], bsz * n_seq)
        return pl.BlockSpec((rows, w.shape[1]), lambda b, s: (jnp.minimum(b * n_seq + s, n_blocks - 1), 0)), rows

    (up_spec, up_rows), (down_spec, down_rows) = side_spec(w_up), side_spec(w_down)
    side_bytes = (up_rows * w_up.shape[1] + down_rows * w_down.shape[1]) * (4 + 2)
    hist_bytes = d * (POOL_HALO + CONV_HALO + 2 * tile) * 4
    return pl.pallas_call(
        functools.partial(_mixer_kernel, tile=tile, d=d),
        out_shape=(jax.ShapeDtypeStruct(x.shape, x.dtype), jax.ShapeDtypeStruct((bsz, n_mod), jnp.float32),
                   jax.ShapeDtypeStruct(w_up.shape, jnp.bfloat16), jax.ShapeDtypeStruct(w_down.shape, jnp.bfloat16)),
        grid=(bsz, n_seq),
        in_specs=[
            pl.BlockSpec((1, tile, d), tok),
            _const_spec((1, d)), _const_spec((1, d)),
            _layer_spec(conv_w, layer), _const_spec((1, d)),
            _const_spec(c.shape), _const_spec((1, n_mod)), _const_spec(w_pool.shape), _const_spec((1, d)),
            up_spec, down_spec,
            hbm, hbm, hbm, hbm,
        ],
        out_specs=(pl.BlockSpec((1, tile, d), tok), pl.BlockSpec((bsz, n_mod), lambda b, s: (0, 0)),
                   up_spec, down_spec),
        scratch_shapes=[pltpu.VMEM((d // V7X_LANES, POOL_HALO + tile, V7X_LANES), jnp.float32),
                        pltpu.VMEM((d // V7X_LANES, CONV_HALO + tile, V7X_LANES), jnp.float32),
                        pltpu.VMEM((d, d), jnp.bfloat16),
                        pltpu.VMEM((d, d_rest), jnp.bfloat16), pltpu.VMEM(w_bout.shape, jnp.bfloat16),
                        pltpu.VMEM(w_o.shape, jnp.bfloat16),
                        pltpu.VMEM((WEIGHT_LOAD_SLOTS, d, WEIGHT_LOAD_COLS), jnp.float32),
                        pltpu.SemaphoreType.DMA((WEIGHT_LOAD_SLOTS,))],
        compiler_params=pltpu.CompilerParams(
            dimension_semantics=("arbitrary", "arbitrary"),
            vmem_limit_bytes=_vmem_limit(weights + hist_bytes, 2 * tile * d * 4 + side_bytes, 8 * tile * d * 4)),
        name="mixer",
    )(x, g_pre.reshape(1, d), g_post.reshape(1, d), conv_w, conv_b.reshape(1, d),
      c, b_ada.reshape(1, n_mod), w_pool, pool_scale.reshape(1, d), w_up, w_down, w_ada, w_in, w_bout, w_o)


def _ffn_kernel(x_ref, mod_ref, gpre_ref, gpost_ref, wup_ref, cw_ref, cb_ref, wdown_ref, o_ref, up_buf,
                *, tile, d, f, chunks):
    s = pl.program_id(1)

    @pl.when(s == 0)
    def _():
        _zero_history(up_buf, CONV_HALO)

    x = x_ref[0]
    sh2, sc2, gt2 = (mod_ref[pl.ds(pl.program_id(0), 1), i * d:(i + 1) * d] for i in range(3, N_MOD))
    hb = (_rmsnorm(x, gpre_ref[...] * (1.0 + sc2)) + sh2).astype(jnp.bfloat16)

    def conv_cols(lo, n, scale=None):
        cols = slice(lo, lo + n)
        up = _dot(hb, wup_ref[:, cols])
        slabs = range(lo // V7X_LANES, (lo + n) // V7X_LANES)
        _put_slabs(up_buf, CONV_HALO, up, slabs[0])
        u1 = _rows_back(up_buf, CONV_HALO, tile, slabs, 1)
        u2 = _rows_back(up_buf, CONV_HALO, tile, slabs, 2)
        _keep_history(up_buf, CONV_HALO, tile, slabs)
        cw, cb = cw_ref[0, :, cols], cb_ref[:, cols]
        if scale is not None:
            cw, cb = cw * scale, cb * scale
        return cb + cw[0:1] * u2 + cw[1:2] * u1 + cw[2:3] * up

    ff = None
    for lo, n in chunks:
        gate = conv_cols(lo, n)
        half_val = conv_cols(f + lo, n, scale=0.5)
        p = gate * half_val
        t = jnp.tanh(gate * (GELU_K + (GELU_K * GELU_A) * (gate * gate)))
        act = (p + p * t).astype(jnp.bfloat16)
        part = _dot(act, wdown_ref[lo:lo + n, :])
        ff = part if ff is None else ff + part
    o_ref[0] = x + _rmsnorm(ff, gpost_ref[...] * gt2)


def _ffn(x, mod, g_pre, g_post, w_up, conv_w, layer, conv_b, w_down):
    bsz, seq, d = x.shape
    tile = FFN_TILE
    f = w_down.shape[0]
    assert sum(FFN_CHUNKS) == f
    chunks = tuple((sum(FFN_CHUNKS[:i]), n) for i, n in enumerate(FFN_CHUNKS))
    weights = 2 * (w_up.size + w_down.size)
    up_buf_bytes = 2 * f * (CONV_HALO + tile) * 4
    tok = lambda b, s: (b, s, 0)
    return pl.pallas_call(
        functools.partial(_ffn_kernel, tile=tile, d=d, f=f, chunks=chunks),
        out_shape=jax.ShapeDtypeStruct(x.shape, x.dtype),
        grid=(bsz, seq // tile),
        in_specs=[
            pl.BlockSpec((1, tile, d), tok),
            _const_spec(mod.shape),
            _const_spec((1, d)), _const_spec((1, d)),
            _const_spec(w_up.shape), _layer_spec(conv_w, layer), _const_spec((1, 2 * f)),
            _const_spec(w_down.shape),
        ],
        out_specs=pl.BlockSpec((1, tile, d), tok),
        scratch_shapes=[pltpu.VMEM((2 * f // V7X_LANES, CONV_HALO + tile, V7X_LANES), jnp.float32)],
        compiler_params=pltpu.CompilerParams(
            dimension_semantics=("arbitrary", "arbitrary"),
            vmem_limit_bytes=max(_vmem_limit(weights + up_buf_bytes, 2 * tile * d * 4, 4 * tile * d * 4),
                                 V7X_SCOPED_VMEM_MAX_BYTES)),
        name="convffn",
    )(x, mod, g_pre.reshape(1, d), g_post.reshape(1, d), w_up, conv_w, conv_b.reshape(1, 2 * f), w_down)


def kernel(x, c, g_pre_mix, g_post_mix, g_pre_ffn, g_post_ffn, w_ada, b_ada, w_in, w_pool, pool_scale, conv_w,
           conv_b, w_bout, w_o, w_up, ffn_conv_w, ffn_conv_b, w_down):
    for l in range(w_ada.shape[0]):
        x, mod, w_up_bf16, w_down_bf16 = _mixer(x, c, w_ada[l], b_ada[l], g_pre_mix[l], g_post_mix[l], w_in[l],
                                                w_pool[l], pool_scale[l], conv_w, l, conv_b[l], w_bout[l], w_o[l],
                                                w_up[l], w_down[l])
        x = _ffn(x, mod, g_pre_ffn[l], g_post_ffn[l], w_up_bf16, ffn_conv_w, l, ffn_conv_b[l], w_down_bf16)
    return x
```

```python
import functools

import jax
import jax.numpy as jnp
from jax import lax
from jax.experimental import pallas as pl
from jax.experimental.pallas import tpu as pltpu

POOL_WINDOWS = (2, 4, 8, 16)
N_MOD = 6
EPS = 1e-6
GELU_K = 0.7978845608028654
GELU_A = 0.044715

V7X_LANES = 128
V7X_BF16_SUBLANES = 16
V7X_VMEM_BYTES = 64 * 1024 * 1024
V7X_SCOPED_VMEM_MAX_BYTES = 60000 * 1024

POOL_HALO = 16
CONV_HALO = 8

MIXER_TILE = 512
FFN_TILE = 512
WEIGHT_LOAD_COLS = 512
WEIGHT_LOAD_SLOTS = 4
ROW_SPLIT = 2
FFN_CHUNKS = (1536, 1280)


def _vmem_limit(resident_bytes, streamed_bytes, temp_bytes):
    need = resident_bytes + 2 * streamed_bytes + temp_bytes
    assert need < V7X_VMEM_BYTES, need
    return int(need)


def _const_spec(shape):
    zeros = (0,) * len(shape)
    return pl.BlockSpec(shape, lambda *_: zeros, pipeline_mode=pl.Buffered(1))


def _layer_spec(stacked, layer):
    zeros = (0,) * (stacked.ndim - 1)
    return pl.BlockSpec((1,) + stacked.shape[1:], lambda *_: (layer,) + zeros, pipeline_mode=pl.Buffered(1))


def _rmsnorm(xf, gain_row):
    return xf * lax.rsqrt(jnp.mean(xf * xf, axis=-1, keepdims=True) + EPS) * gain_row


def _put_slabs(buf, halo, val, first):
    for i in range(val.shape[1] // V7X_LANES):
        buf[first + i, halo:, :] = val[:, i * V7X_LANES:(i + 1) * V7X_LANES]


def _rows_back(buf, halo, tile, slabs, k):
    return jnp.concatenate([buf[j, halo - k:halo - k + tile, :] for j in slabs], axis=1)


def _keep_history(buf, halo, tile, slabs):
    for j in slabs:
        buf[j, :halo, :] = buf[j, tile:, :]


def _zero_history(buf, halo):
    buf[:, :halo, :] = jnp.zeros((buf.shape[0], halo, V7X_LANES), jnp.float32)


def _dot(a, b):
    return jnp.dot(a, b, preferred_element_type=jnp.float32)


def _row_blocks(rows, steps):
    for n in range(steps, 0, -1):
        if rows % n == 0 and (rows // n) % V7X_BF16_SUBLANES == 0:
            return n, rows // n
    raise ValueError((rows, steps))


def _mixer_kernel(x_ref, gpre_ref, gpost_ref, convw_ref, convb_ref, c_ref, bada_ref, wpool_ref, pscale_ref,
                  wup_f32, wdown_f32, wada_hbm, win_hbm, wbout_hbm, wo_hbm,
                  o_ref, mod_ref, wup_bf16, wdown_bf16,
                  pool_buf, conv_buf, wfold_ref, win_ref, wbout_ref, wo_ref, stage, stage_sem, *, tile, d):
    b, s = pl.program_id(0), pl.program_id(1)
    n_slabs = d // V7X_LANES
    is_first = (b == 0) & (s == 0)
    cols, n_slots = stage.shape[2], stage.shape[0]
    gw = wpool_ref.shape[1]


    def ada_cols(c0):
        def use(chunk):
            c_bf16 = c_ref[...].astype(jnp.bfloat16)
            mod_ref[:, c0:c0 + cols] = _dot(c_bf16, chunk.astype(jnp.bfloat16)) + bada_ref[:, c0:c0 + cols]
        return use

    def fold_cols(c0):
        def use(chunk):
            for g in range(c0 // gw, (c0 + cols) // gw):
                gcols = slice(g * gw, (g + 1) * gw)
                wp = wpool_ref[g] * (0.5 * pscale_ref[:, gcols])
                wfold = jnp.dot(chunk[:, g * gw - c0:(g + 1) * gw - c0], wp, precision=lax.Precision.HIGHEST,
                                preferred_element_type=jnp.float32)
                wfold_ref[:, gcols] = wfold.astype(jnp.bfloat16)
        return use

    def cast_into(dst):
        def use(chunk):
            dst[...] = chunk.astype(jnp.bfloat16)
        return use

    moves = [(wada_hbm.at[:, pl.ds(c0, cols)], ada_cols(c0)) for c0 in range(0, wada_hbm.shape[1], cols)]
    moves += [(win_hbm.at[:, pl.ds(c0, cols)], fold_cols(c0)) for c0 in range(0, d, cols)]
    n_before_step = len(moves)
    later = {}
    for slab in (3, 1, 2, 4, 5):
        later["w_in slab %d" % slab] = list(range(len(moves), len(moves) + d // cols))
        moves += [(win_hbm.at[:, pl.ds(c0, cols)], cast_into(win_ref.at[:, pl.ds(c0 - d, cols)]))
                  for c0 in range(slab * d, (slab + 1) * d, cols)]
    for name, src, dst in (("w_bout", wbout_hbm, wbout_ref), ("w_o", wo_hbm, wo_ref)):
        later[name] = list(range(len(moves), len(moves) + d // cols))
        moves += [(src.at[:, pl.ds(c0, cols)], cast_into(dst.at[:, pl.ds(c0, cols)])) for c0 in range(0, d, cols)]

    def load(i):
        return pltpu.make_async_copy(moves[i][0], stage.at[i % n_slots], stage_sem.at[i % n_slots])

    def consume(i):
        if i + n_slots - 1 < len(moves):
            load(i + n_slots - 1).start()
        load(i).wait()
        moves[i][1](stage[i % n_slots])


    def step(first):
        order = list(later)
        fetched = []

        def need(name):
            if first:
                upto = min(order.index(name) + 1, len(order) - 1)
                for nxt in order[len(fetched):upto + 1]:
                    for i in later[nxt]:
                        consume(i)
                    fetched.append(nxt)

        wup_bf16[...] = wup_f32[...].astype(jnp.bfloat16)
        wdown_bf16[...] = wdown_f32[...].astype(jnp.bfloat16)

        @pl.when(s == 0)
        def _():
            _zero_history(pool_buf, POOL_HALO)
            _zero_history(conv_buf, CONV_HALO)

        x = x_ref[0]
        sh1, sc1, gt1 = (mod_ref[pl.ds(b, 1), i * d:(i + 1) * d] for i in range(3))
        halves = [slice(i * tile // ROW_SPLIT, (i + 1) * tile // ROW_SPLIT) for i in range(ROW_SPLIT)]
        pre_gain = gpre_ref[...] * (1.0 + sc1)
        hb_parts = [(_rmsnorm(x[rows], pre_gain) + sh1).astype(jnp.bfloat16) for rows in halves]
        hb = jnp.concatenate(hb_parts, axis=0)

        def proj(i):
            need("w_in slab %d" % i)
            return _dot(hb, win_ref[:, (i - 1) * d:i * d])

        u = jnp.concatenate([_dot(part, wfold_ref[...]) for part in hb_parts], axis=0)
        _put_slabs(pool_buf, POOL_HALO, u, 0)
        uc = proj(3)
        v = uc * proj(1)
        _put_slabs(conv_buf, CONV_HALO, v, 0)

        t1 = (s * tile + lax.broadcasted_iota(jnp.int32, (tile, V7X_LANES), 0) + 1).astype(jnp.float32)
        half_ya = []
        for g, w in enumerate(POOL_WINDOWS):
            slabs = range(g * gw // V7X_LANES, (g + 1) * gw // V7X_LANES)
            ug = u[:, g * gw:(g + 1) * gw]
            acc = ug
            for k in range(1, w):
                acc = acc + _rows_back(pool_buf, POOL_HALO, tile, slabs, k)
            inv = 1.0 / jnp.minimum(t1, float(w))
            inv = jnp.concatenate([inv] * len(slabs), axis=1)
            half_ya.append(acc * inv - ug)
        _keep_history(pool_buf, POOL_HALO, tile, range(n_slabs))
        half_ya = jnp.concatenate(half_ya, axis=1)

        ub = proj(2)

        v1 = _rows_back(conv_buf, CONV_HALO, tile, range(n_slabs), 1)
        v2 = _rows_back(conv_buf, CONV_HALO, tile, range(n_slabs), 2)
        _keep_history(conv_buf, CONV_HALO, tile, range(n_slabs))
        cw, cb = 0.5 * convw_ref[0], 0.5 * convb_ref[...]
        half_conv = cb + cw[0:1] * v2 + cw[1:2] * v1 + cw[2:3] * v
        half_ubc = (ub * half_conv).astype(jnp.bfloat16)

        za = proj(4)
        zb = proj(5)
        gated_a = half_ya + half_ya * jnp.tanh(0.5 * za)
        need("w_bout")
        half_yb = _dot(half_ubc, wbout_ref[...])
        merged = (gated_a + (half_yb + half_yb * jnp.tanh(0.5 * zb))).astype(jnp.bfloat16)
        post_gain = gpost_ref[...] * gt1
        need("w_o")
        for rows in halves:
            o_ref[0, rows, :] = x[rows] + _rmsnorm(_dot(merged[rows], wo_ref[...]), post_gain)

    @pl.when(is_first)
    def _():
        for i in range(n_slots - 1):
            load(i).start()
        for i in range(n_before_step):
            consume(i)
        step(first=True)

    @pl.when(jnp.logical_not(is_first))
    def _():
        step(first=False)


def _mixer(x, c, w_ada, b_ada, g_pre, g_post, w_in, w_pool, pool_scale, conv_w, layer, conv_b, w_bout, w_o, w_up,
           w_down):
    bsz, seq, d = x.shape
    tile = MIXER_TILE
    n_seq = seq // tile
    n_mod = w_ada.shape[1]
    d_rest = w_in.shape[1] - d
    gw = w_pool.shape[1]
    assert d_rest % WEIGHT_LOAD_COLS == 0 and d % WEIGHT_LOAD_COLS == 0 and n_mod % WEIGHT_LOAD_COLS == 0
    assert WEIGHT_LOAD_COLS % gw == 0 and w_ada.shape[0] == d
    weights = (2 * (d * d + d * d_rest + w_bout.size + w_o.size) + WEIGHT_LOAD_SLOTS * d * WEIGHT_LOAD_COLS * 4
               + w_pool.size * 4 + 2 * bsz * n_mod * 4)
    tok = lambda b, s: (b, s, 0)
    hbm = pl.BlockSpec(memory_space=pl.ANY)

    def side_spec(w):
        n_blocks, rows = _row_blocks(w.shape[0], bsz * n_seq)
        return pl.BlockSpec((rows, w.shape[1]), lambda b, s: (jnp.minimum(b * n_seq + s, n_blocks - 1), 0)), rows

    (up_spec, up_rows), (down_spec, down_rows) = side_spec(w_up), side_spec(w_down)
    side_bytes = (up_rows * w_up.shape[1] + down_rows * w_down.shape[1]) * (4 + 2)
    hist_bytes = d * (POOL_HALO + CONV_HALO + 2 * tile) * 4
    return pl.pallas_call(
        functools.partial(_mixer_kernel, tile=tile, d=d),
        out_shape=(jax.ShapeDtypeStruct(x.shape, x.dtype), jax.ShapeDtypeStruct((bsz, n_mod), jnp.float32),
                   jax.ShapeDtypeStruct(w_up.shape, jnp.bfloat16), jax.ShapeDtypeStruct(w_down.shape, jnp.bfloat16)),
        grid=(bsz, n_seq),
        in_specs=[
            pl.BlockSpec((1, tile, d), tok),
            _const_spec((1, d)), _const_spec((1, d)),
            _layer_spec(conv_w, layer), _const_spec((1, d)),
            _const_spec(c.shape), _const_spec((1, n_mod)), _const_spec(w_pool.shape), _const_spec((1, d)),
            up_spec, down_spec,
            hbm, hbm, hbm, hbm,
        ],
        out_specs=(pl.BlockSpec((1, tile, d), tok), pl.BlockSpec((bsz, n_mod), lambda b, s: (0, 0)),
                   up_spec, down_spec),
        scratch_shapes=[pltpu.VMEM((d // V7X_LANES, POOL_HALO + tile, V7X_LANES), jnp.float32),
                        pltpu.VMEM((d // V7X_LANES, CONV_HALO + tile, V7X_LANES), jnp.float32),
                        pltpu.VMEM((d, d), jnp.bfloat16),
                        pltpu.VMEM((d, d_rest), jnp.bfloat16), pltpu.VMEM(w_bout.shape, jnp.bfloat16),
                        pltpu.VMEM(w_o.shape, jnp.bfloat16),
                        pltpu.VMEM((WEIGHT_LOAD_SLOTS, d, WEIGHT_LOAD_COLS), jnp.float32),
                        pltpu.SemaphoreType.DMA((WEIGHT_LOAD_SLOTS,))],
        compiler_params=pltpu.CompilerParams(
            dimension_semantics=("arbitrary", "arbitrary"),
            vmem_limit_bytes=_vmem_limit(weights + hist_bytes, 2 * tile * d * 4 + side_bytes, 8 * tile * d * 4)),
        name="mixer",
    )(x, g_pre.reshape(1, d), g_post.reshape(1, d), conv_w, conv_b.reshape(1, d),
      c, b_ada.reshape(1, n_mod), w_pool, pool_scale.reshape(1, d), w_up, w_down, w_ada, w_in, w_bout, w_o)


def _ffn_kernel(x_ref, mod_ref, gpre_ref, gpost_ref, wup_ref, cw_ref, cb_ref, wdown_ref, o_ref, up_buf,
                *, tile, d, f, chunks):
    s = pl.program_id(1)

    @pl.when(s == 0)
    def _():
        _zero_history(up_buf, CONV_HALO)

    x = x_ref[0]
    sh2, sc2, gt2 = (mod_ref[pl.ds(pl.program_id(0), 1), i * d:(i + 1) * d] for i in range(3, N_MOD))
    hb = (_rmsnorm(x, gpre_ref[...] * (1.0 + sc2)) + sh2).astype(jnp.bfloat16)

    def conv_cols(lo, n, scale=None):
        cols = slice(lo, lo + n)
        up = _dot(hb, wup_ref[:, cols])
        slabs = range(lo // V7X_LANES, (lo + n) // V7X_LANES)
        _put_slabs(up_buf, CONV_HALO, up, slabs[0])
        u1 = _rows_back(up_buf, CONV_HALO, tile, slabs, 1)
        u2 = _rows_back(up_buf, CONV_HALO, tile, slabs, 2)
        _keep_history(up_buf, CONV_HALO, tile, slabs)
        cw, cb = cw_ref[0, :, cols], cb_ref[:, cols]
        if scale is not None:
            cw, cb = cw * scale, cb * scale
        return cb + cw[0:1] * u2 + cw[1:2] * u1 + cw[2:3] * up

    ff = None
    for lo, n in chunks:
        gate = conv_cols(lo, n)
        half_val = conv_cols(f + lo, n, scale=0.5)
        p = gate * half_val
        t = jnp.tanh(gate * (GELU_K + (GELU_K * GELU_A) * (gate * gate)))
        act = (p + p * t).astype(jnp.bfloat16)
        part = _dot(act, wdown_ref[lo:lo + n, :])
        ff = part if ff is None else ff + part
    o_ref[0] = x + _rmsnorm(ff, gpost_ref[...] * gt2)


def _ffn(x, mod, g_pre, g_post, w_up, conv_w, layer, conv_b, w_down):
    bsz, seq, d = x.shape
    tile = FFN_TILE
    f = w_down.shape[0]
    assert sum(FFN_CHUNKS) == f
    chunks = tuple((sum(FFN_CHUNKS[:i]), n) for i, n in enumerate(FFN_CHUNKS))
    weights = 2 * (w_up.size + w_down.size)
    up_buf_bytes = 2 * f * (CONV_HALO + tile) * 4
    tok = lambda b, s: (b, s, 0)
    return pl.pallas_call(
        functools.partial(_ffn_kernel, tile=tile, d=d, f=f, chunks=chunks),
        out_shape=jax.ShapeDtypeStruct(x.shape, x.dtype),
        grid=(bsz, seq // tile),
        in_specs=[
            pl.BlockSpec((1, tile, d), tok),
            _const_spec(mod.shape),
            _const_spec((1, d)), _const_spec((1, d)),
            _const_spec(w_up.shape), _layer_spec(conv_w, layer), _const_spec((1, 2 * f)),
            _const_spec(w_down.shape),
        ],
        out_specs=pl.BlockSpec((1, tile, d), tok),
        scratch_shapes=[pltpu.VMEM((2 * f // V7X_LANES, CONV_HALO + tile, V7X_LANES), jnp.float32)],
        compiler_params=pltpu.CompilerParams(
            dimension_semantics=("arbitrary", "arbitrary"),
            vmem_limit_bytes=max(_vmem_limit(weights + up_buf_bytes, 2 * tile * d * 4, 4 * tile * d * 4),
                                 V7X_SCOPED_VMEM_MAX_BYTES)),
        name="convffn",
    )(x, mod, g_pre.reshape(1, d), g_post.reshape(1, d), w_up, conv_w, conv_b.reshape(1, 2 * f), w_down)


def kernel(x, c, g_pre_mix, g_post_mix, g_pre_ffn, g_post_ffn, w_ada, b_ada, w_in, w_pool, pool_scale, conv_w,
           conv_b, w_bout, w_o, w_up, ffn_conv_w, ffn_conv_b, w_down):
    for l in range(w_ada.shape[0]):
        x, mod, w_up_bf16, w_down_bf16 = _mixer(x, c, w_ada[l], b_ada[l], g_pre_mix[l], g_post_mix[l], w_in[l],
                                                w_pool[l], pool_scale[l], conv_w, l, conv_b[l], w_bout[l], w_o[l],
                                                w_up[l], w_down[l])
        x = _ffn(x, mod, g_pre_ffn[l], g_post_ffn[l], w_up_bf16, ffn_conv_w, l, ffn_conv_b[l], w_down_bf16)
    return x
```

```python
import functools

import jax
import jax.numpy as jnp
from jax import lax
from jax.experimental import pallas as pl
from jax.experimental.pallas import tpu as pltpu

POOL_WINDOWS = (2, 4, 8, 16)
N_MOD = 6
EPS = 1e-6
GELU_K = 0.7978845608028654
GELU_A = 0.044715

V7X_LANES = 128
V7X_SUBLANES = 8
V7X_BF16_SUBLANES = 16
V7X_VMEM_BYTES = 64 * 1024 * 1024
V7X_SCOPED_VMEM_MAX_BYTES = 60000 * 1024

POOL_HALO = 16
CONV_HALO = 8

MIXER_TILE = 512
FFN_TILE = 512
WEIGHT_LOAD_COLS = 512
WEIGHT_LOAD_SLOTS = 4
ROW_SPLIT = 2
FFN_CHUNKS = (1536, 1280)


def _vmem_limit(resident_bytes, streamed_bytes, temp_bytes):
    need = resident_bytes + 2 * streamed_bytes + temp_bytes
    assert need < V7X_VMEM_BYTES, need
    return int(need)


def _const_spec(shape):
    zeros = (0,) * len(shape)
    return pl.BlockSpec(shape, lambda *_: zeros, pipeline_mode=pl.Buffered(1))


def _layer_spec(stacked, layer):
    zeros = (0,) * (stacked.ndim - 1)
    return pl.BlockSpec((1,) + stacked.shape[1:], lambda *_: (layer,) + zeros, pipeline_mode=pl.Buffered(1))


def _rmsnorm(xf, gain_row):
    return xf * lax.rsqrt(jnp.mean(xf * xf, axis=-1, keepdims=True) + EPS) * gain_row


def _put_slabs(buf, halo, val, first):
    for i in range(val.shape[1] // V7X_LANES):
        buf[first + i, halo:, :] = val[:, i * V7X_LANES:(i + 1) * V7X_LANES]


def _rows_back(buf, halo, tile, slabs, k):
    return jnp.concatenate([buf[j, halo - k:halo - k + tile, :] for j in slabs], axis=1)


def _keep_history(buf, halo, tile, slabs):
    for j in slabs:
        buf[j, :halo, :] = buf[j, tile:, :]


def _zero_history(buf, halo):
    buf[:, :halo, :] = jnp.zeros((buf.shape[0], halo, V7X_LANES), jnp.float32)


def _dot(a, b):
    return jnp.dot(a, b, preferred_element_type=jnp.float32)


def _row_blocks(rows, steps):
    for n in range(steps, 0, -1):
        if rows % n == 0 and (rows // n) % V7X_BF16_SUBLANES == 0:
            return n, rows // n
    raise ValueError((rows, steps))


def _mixer_kernel(x_ref, gpre_ref, gpost_ref, convw_ref, convb_ref, c_ref, bada_ref, wpool_ref, pscale_ref,
                  wup_f32, wdown_f32, wada_hbm, win_hbm, wbout_hbm, wo_hbm,
                  o_ref, mod_ref, wup_bf16, wdown_bf16,
                  pool_buf, conv_buf, wfold_ref, win_ref, wbout_ref, wo_ref, stage, stage_sem, *, tile, d):
    b, s = pl.program_id(0), pl.program_id(1)
    n_slabs = d // V7X_LANES
    is_first = (b == 0) & (s == 0)
    cols, n_slots = stage.shape[2], stage.shape[0]
    gw = wpool_ref.shape[1]


    def ada_cols(c0):
        def use(chunk):
            c_bf16 = c_ref[...].astype(jnp.bfloat16)
            mod_ref[:, c0:c0 + cols] = _dot(c_bf16, chunk.astype(jnp.bfloat16)) + bada_ref[:, c0:c0 + cols]
        return use

    def fold_cols(c0):
        def use(chunk):
            for g in range(c0 // gw, (c0 + cols) // gw):
                gcols = slice(g * gw, (g + 1) * gw)
                wp = wpool_ref[g] * (0.5 * pscale_ref[:, gcols])
                wfold = jnp.dot(chunk[:, g * gw - c0:(g + 1) * gw - c0], wp, precision=lax.Precision.HIGHEST,
                                preferred_element_type=jnp.float32)
                wfold_ref[:, gcols] = wfold.astype(jnp.bfloat16)
        return use

    def cast_into(dst):
        def use(chunk):
            dst[...] = chunk.astype(jnp.bfloat16)
        return use

    moves = [(wada_hbm.at[:, pl.ds(c0, cols)], ada_cols(c0)) for c0 in range(0, wada_hbm.shape[1], cols)]
    moves += [(win_hbm.at[:, pl.ds(c0, cols)], fold_cols(c0)) for c0 in range(0, d, cols)]
    n_before_step = len(moves)
    later = {}
    for slab in (3, 1, 2, 4, 5):
        later["w_in slab %d" % slab] = list(range(len(moves), len(moves) + d // cols))
        moves += [(win_hbm.at[:, pl.ds(c0, cols)], cast_into(win_ref.at[:, pl.ds(c0 - d, cols)]))
                  for c0 in range(slab * d, (slab + 1) * d, cols)]
    for name, src, dst in (("w_bout", wbout_hbm, wbout_ref), ("w_o", wo_hbm, wo_ref)):
        later[name] = list(range(len(moves), len(moves) + d // cols))
        moves += [(src.at[:, pl.ds(c0, cols)], cast_into(dst.at[:, pl.ds(c0, cols)])) for c0 in range(0, d, cols)]

    def load(i):
        return pltpu.make_async_copy(moves[i][0], stage.at[i % n_slots], stage_sem.at[i % n_slots])

    def consume(i):
        if i + n_slots - 1 < len(moves):
            load(i + n_slots - 1).start()
        load(i).wait()
        moves[i][1](stage[i % n_slots])


    def step(first):
        order = list(later)
        fetched = []

        def need(name):
            if first:
                upto = min(order.index(name) + 1, len(order) - 1)
                for nxt in order[len(fetched):upto + 1]:
                    for i in later[nxt]:
                        consume(i)
                    fetched.append(nxt)

        wup_bf16[...] = wup_f32[...].astype(jnp.bfloat16)
        wdown_bf16[...] = wdown_f32[...].astype(jnp.bfloat16)

        @pl.when(s == 0)
        def _():
            _zero_history(pool_buf, POOL_HALO)
            _zero_history(conv_buf, CONV_HALO)

        x = x_ref[0]
        sh1, sc1, gt1 = (mod_ref[pl.ds(b, 1), i * d:(i + 1) * d] for i in range(3))
        halves = [slice(i * tile // ROW_SPLIT, (i + 1) * tile // ROW_SPLIT) for i in range(ROW_SPLIT)]
        pre_gain = gpre_ref[...] * (1.0 + sc1)
        hb_parts = [(_rmsnorm(x[rows], pre_gain) + sh1).astype(jnp.bfloat16) for rows in halves]
        hb = jnp.concatenate(hb_parts, axis=0)

        def proj(i):
            need("w_in slab %d" % i)
            return _dot(hb, win_ref[:, (i - 1) * d:i * d])

        u = jnp.concatenate([_dot(part, wfold_ref[...]) for part in hb_parts], axis=0)
        _put_slabs(pool_buf, POOL_HALO, u, 0)
        uc = proj(3)
        v = uc * proj(1)
        _put_slabs(conv_buf, CONV_HALO, v, 0)

        t1 = (s * tile + lax.broadcasted_iota(jnp.int32, (tile, V7X_LANES), 0) + 1).astype(jnp.float32)
        half_ya = []
        for g, w in enumerate(POOL_WINDOWS):
            slabs = range(g * gw // V7X_LANES, (g + 1) * gw // V7X_LANES)
            ug = u[:, g * gw:(g + 1) * gw]
            if w % (2 * V7X_SUBLANES) == 0:
                hw = w // 2
                part = _rows_back(pool_buf, POOL_HALO - hw, tile + hw, slabs, 0)
                for k in range(1, hw):
                    part = part + _rows_back(pool_buf, POOL_HALO - hw, tile + hw, slabs, k)
                acc = part[hw:] + part[:tile]
            else:
                acc = ug
                for k in range(1, w):
                    acc = acc + _rows_back(pool_buf, POOL_HALO, tile, slabs, k)
            inv = 1.0 / jnp.minimum(t1, float(w))
            inv = jnp.concatenate([inv] * len(slabs), axis=1)
            half_ya.append(acc * inv - ug)
        _keep_history(pool_buf, POOL_HALO, tile, range(n_slabs))
        half_ya = jnp.concatenate(half_ya, axis=1)

        ub = proj(2)

        v1 = _rows_back(conv_buf, CONV_HALO, tile, range(n_slabs), 1)
        v2 = _rows_back(conv_buf, CONV_HALO, tile, range(n_slabs), 2)
        _keep_history(conv_buf, CONV_HALO, tile, range(n_slabs))
        cw, cb = 0.5 * convw_ref[0], 0.5 * convb_ref[...]
        half_conv = cb + cw[0:1] * v2 + cw[1:2] * v1 + cw[2:3] * v
        half_ubc = (ub * half_conv).astype(jnp.bfloat16)

        za = proj(4)
        zb = proj(5)
        gated_a = half_ya + half_ya * jnp.tanh(0.5 * za)
        need("w_bout")
        half_yb = _dot(half_ubc, wbout_ref[...])
        merged = (gated_a + (half_yb + half_yb * jnp.tanh(0.5 * zb))).astype(jnp.bfloat16)
        post_gain = gpost_ref[...] * gt1
        need("w_o")
        for rows in halves:
            o_ref[0, rows, :] = x[rows] + _rmsnorm(_dot(merged[rows], wo_ref[...]), post_gain)

    @pl.when(is_first)
    def _():
        for i in range(n_slots - 1):
            load(i).start()
        for i in range(n_before_step):
            consume(i)
        step(first=True)

    @pl.when(jnp.logical_not(is_first))
    def _():
        step(first=False)


def _mixer(x, c, w_ada, b_ada, g_pre, g_post, w_in, w_pool, pool_scale, conv_w, layer, conv_b, w_bout, w_o, w_up,
           w_down):
    bsz, seq, d = x.shape
    tile = MIXER_TILE
    n_seq = seq // tile
    n_mod = w_ada.shape[1]
    d_rest = w_in.shape[1] - d
    gw = w_pool.shape[1]
    assert d_rest % WEIGHT_LOAD_COLS == 0 and d % WEIGHT_LOAD_COLS == 0 and n_mod % WEIGHT_LOAD_COLS == 0
    assert WEIGHT_LOAD_COLS % gw == 0 and w_ada.shape[0] == d
    weights = (2 * (d * d + d * d_rest + w_bout.size + w_o.size) + WEIGHT_LOAD_SLOTS * d * WEIGHT_LOAD_COLS * 4
               + w_pool.size * 4 + 2 * bsz * n_mod * 4)
    tok = lambda b, s: (b, s, 0)
    hbm = pl.BlockSpec(memory_space=pl.ANY)

    def side_spec(w):
        n_blocks, rows = _row_blocks(w.shape[0], bsz * n_seq)
        return pl.BlockSpec((rows, w.shape[1]), lambda b, s: (jnp.minimum(b * n_seq + s, n_blocks - 1), 0)), rows

    (up_spec, up_rows), (down_spec, down_rows) = side_spec(w_up), side_spec(w_down)
    side_bytes = (up_rows * w_up.shape[1] + down_rows * w_down.shape[1]) * (4 + 2)
    hist_bytes = d * (POOL_HALO + CONV_HALO + 2 * tile) * 4
    return pl.pallas_call(
        functools.partial(_mixer_kernel, tile=tile, d=d),
        out_shape=(jax.ShapeDtypeStruct(x.shape, x.dtype), jax.ShapeDtypeStruct((bsz, n_mod), jnp.float32),
                   jax.ShapeDtypeStruct(w_up.shape, jnp.bfloat16), jax.ShapeDtypeStruct(w_down.shape, jnp.bfloat16)),
        grid=(bsz, n_seq),
        in_specs=[
            pl.BlockSpec((1, tile, d), tok),
            _const_spec((1, d)), _const_spec((1, d)),
            _layer_spec(conv_w, layer), _const_spec((1, d)),
            _const_spec(c.shape), _const_spec((1, n_mod)), _const_spec(w_pool.shape), _const_spec((1, d)),
            up_spec, down_spec,
            hbm, hbm, hbm, hbm,
        ],
        out_specs=(pl.BlockSpec((1, tile, d), tok), pl.BlockSpec((bsz, n_mod), lambda b, s: (0, 0)),
                   up_spec, down_spec),
        scratch_shapes=[pltpu.VMEM((d // V7X_LANES, POOL_HALO + tile, V7X_LANES), jnp.float32),
                        pltpu.VMEM((d // V7X_LANES, CONV_HALO + tile, V7X_LANES), jnp.float32),
                        pltpu.VMEM((d, d), jnp.bfloat16),
                        pltpu.VMEM((d, d_rest), jnp.bfloat16), pltpu.VMEM(w_bout.shape, jnp.bfloat16),
                        pltpu.VMEM(w_o.shape, jnp.bfloat16),
                        pltpu.VMEM((WEIGHT_LOAD_SLOTS, d, WEIGHT_LOAD_COLS), jnp.float32),
                        pltpu.SemaphoreType.DMA((WEIGHT_LOAD_SLOTS,))],
        compiler_params=pltpu.CompilerParams(
            dimension_semantics=("arbitrary", "arbitrary"),
            vmem_limit_bytes=_vmem_limit(weights + hist_bytes, 2 * tile * d * 4 + side_bytes, 8 * tile * d * 4)),
        name="mixer",
    )(x, g_pre.reshape(1, d), g_post.reshape(1, d), conv_w, conv_b.reshape(1, d),
      c, b_ada.reshape(1, n_mod), w_pool, pool_scale.reshape(1, d), w_up, w_down, w_ada, w_in, w_bout, w_o)


def _ffn_kernel(x_ref, mod_ref, gpre_ref, gpost_ref, wup_ref, cw_ref, cb_ref, wdown_ref, o_ref, up_buf,
                *, tile, d, f, chunks):
    s = pl.program_id(1)

    @pl.when(s == 0)
    def _():
        _zero_history(up_buf, CONV_HALO)

    x = x_ref[0]
    sh2, sc2, gt2 = (mod_ref[pl.ds(pl.program_id(0), 1), i * d:(i + 1) * d] for i in range(3, N_MOD))
    hb = (_rmsnorm(x, gpre_ref[...] * (1.0 + sc2)) + sh2).astype(jnp.bfloat16)

    def conv_cols(lo, n, scale=None):
        cols = slice(lo, lo + n)
        up = _dot(hb, wup_ref[:, cols])
        slabs = range(lo // V7X_LANES, (lo + n) // V7X_LANES)
        _put_slabs(up_buf, CONV_HALO, up, slabs[0])
        u1 = _rows_back(up_buf, CONV_HALO, tile, slabs, 1)
        u2 = _rows_back(up_buf, CONV_HALO, tile, slabs, 2)
        _keep_history(up_buf, CONV_HALO, tile, slabs)
        cw, cb = cw_ref[0, :, cols], cb_ref[:, cols]
        if scale is not None:
            cw, cb = cw * scale, cb * scale
        return cb + cw[0:1] * u2 + cw[1:2] * u1 + cw[2:3] * up

    ff = None
    for lo, n in chunks:
        gate = conv_cols(lo, n)
        half_val = conv_cols(f + lo, n, scale=0.5)
        p = gate * half_val
        t = jnp.tanh(gate * (GELU_K + (GELU_K * GELU_A) * (gate * gate)))
        act = (p + p * t).astype(jnp.bfloat16)
        part = _dot(act, wdown_ref[lo:lo + n, :])
        ff = part if ff is None else ff + part
    o_ref[0] = x + _rmsnorm(ff, gpost_ref[...] * gt2)


def _ffn(x, mod, g_pre, g_post, w_up, conv_w, layer, conv_b, w_down):
    bsz, seq, d = x.shape
    tile = FFN_TILE
    f = w_down.shape[0]
    assert sum(FFN_CHUNKS) == f
    chunks = tuple((sum(FFN_CHUNKS[:i]), n) for i, n in enumerate(FFN_CHUNKS))
    weights = 2 * (w_up.size + w_down.size)
    up_buf_bytes = 2 * f * (CONV_HALO + tile) * 4
    tok = lambda b, s: (b, s, 0)
    return pl.pallas_call(
        functools.partial(_ffn_kernel, tile=tile, d=d, f=f, chunks=chunks),
        out_shape=jax.ShapeDtypeStruct(x.shape, x.dtype),
        grid=(bsz, seq // tile),
        in_specs=[
            pl.BlockSpec((1, tile, d), tok),
            _const_spec(mod.shape),
            _const_spec((1, d)), _const_spec((1, d)),
            _const_spec(w_up.shape), _layer_spec(conv_w, layer), _const_spec((1, 2 * f)),
            _const_spec(w_down.shape),
        ],
        out_specs=pl.BlockSpec((1, tile, d), tok),
        scratch_shapes=[pltpu.VMEM((2 * f // V7X_LANES, CONV_HALO + tile, V7X_LANES), jnp.float32)],
        compiler_params=pltpu.CompilerParams(
            dimension_semantics=("arbitrary", "arbitrary"),
            vmem_limit_bytes=max(_vmem_limit(weights + up_buf_bytes, 2 * tile * d * 4, 4 * tile * d * 4),
                                 V7X_SCOPED_VMEM_MAX_BYTES)),
        name="convffn",
    )(x, mod, g_pre.reshape(1, d), g_post.reshape(1, d), w_up, conv_w, conv_b.reshape(1, 2 * f), w_down)


def kernel(x, c, g_pre_mix, g_post_mix, g_pre_ffn, g_post_ffn, w_ada, b_ada, w_in, w_pool, pool_scale, conv_w,
           conv_b, w_bout, w_o, w_up, ffn_conv_w, ffn_conv_b, w_down):
    for l in range(w_ada.shape[0]):
        x, mod, w_up_bf16, w_down_bf16 = _mixer(x, c, w_ada[l], b_ada[l], g_pre_mix[l], g_post_mix[l], w_in[l],
                                                w_pool[l], pool_scale[l], conv_w, l, conv_b[l], w_bout[l], w_o[l],
                                                w_up[l], w_down[l])
        x = _ffn(x, mod, g_pre_ffn[l], g_post_ffn[l], w_up_bf16, ffn_conv_w, l, ffn_conv_b[l], w_down_bf16)
    return x
```

```python
import functools

import jax
import jax.numpy as jnp
from jax import lax
from jax.experimental import pallas as pl
from jax.experimental.pallas import tpu as pltpu

POOL_WINDOWS = (2, 4, 8, 16)
N_MOD = 6
EPS = 1e-6
GELU_K = 0.7978845608028654
GELU_A = 0.044715

V7X_LANES = 128
V7X_SUBLANES = 8
V7X_BF16_SUBLANES = 16
V7X_VMEM_BYTES = 64 * 1024 * 1024
V7X_SCOPED_VMEM_MAX_BYTES = 60000 * 1024

POOL_HALO = 16
CONV_HALO = 8

MIXER_TILE = 512
FFN_TILE = 512
WEIGHT_LOAD_COLS = 512
WEIGHT_LOAD_SLOTS = 4
ROW_SPLIT = 2
FFN_CHUNKS = (1536, 1280)


def _vmem_limit(resident_bytes, streamed_bytes, temp_bytes):
    need = resident_bytes + 2 * streamed_bytes + temp_bytes
    assert need < V7X_VMEM_BYTES, need
    return int(need)


def _const_spec(shape):
    zeros = (0,) * len(shape)
    return pl.BlockSpec(shape, lambda *_: zeros, pipeline_mode=pl.Buffered(1))


def _layer_spec(stacked, layer):
    zeros = (0,) * (stacked.ndim - 1)
    return pl.BlockSpec((1,) + stacked.shape[1:], lambda *_: (layer,) + zeros, pipeline_mode=pl.Buffered(1))


def _rmsnorm(xf, gain_row):
    return xf * lax.rsqrt(jnp.mean(xf * xf, axis=-1, keepdims=True) + EPS) * gain_row


def _put_slabs(buf, halo, val, first):
    for i in range(val.shape[1] // V7X_LANES):
        buf[first + i, halo:, :] = val[:, i * V7X_LANES:(i + 1) * V7X_LANES]


def _rows_back(buf, halo, tile, slabs, k):
    return jnp.concatenate([buf[j, halo - k:halo - k + tile, :] for j in slabs], axis=1)


def _keep_history(buf, halo, tile, slabs):
    for j in slabs:
        buf[j, :halo, :] = buf[j, tile:, :]


def _zero_history(buf, halo):
    buf[:, :halo, :] = jnp.zeros((buf.shape[0], halo, V7X_LANES), jnp.float32)


def _dot(a, b):
    return jnp.dot(a, b, preferred_element_type=jnp.float32)


def _row_blocks(rows, steps):
    for n in range(steps, 0, -1):
        if rows % n == 0 and (rows // n) % V7X_BF16_SUBLANES == 0:
            return n, rows // n
    raise ValueError((rows, steps))


def _mixer_kernel(x_ref, gpre_ref, gpost_ref, convw_ref, convb_ref, c_ref, bada_ref, wpool_ref, pscale_ref,
                  wup_f32, wdown_f32, wada_hbm, win_hbm, wbout_hbm, wo_hbm,
                  o_ref, mod_ref, wup_bf16, wdown_bf16,
                  pool_buf, conv_buf, wfold_ref, win_ref, wbout_ref, wo_ref, stage, stage_sem, *, tile, d):
    b, s = pl.program_id(0), pl.program_id(1)
    n_slabs = d // V7X_LANES
    is_first = (b == 0) & (s == 0)
    cols, n_slots = stage.shape[2], stage.shape[0]
    gw = wpool_ref.shape[1]


    def ada_cols(c0):
        def use(chunk):
            c_bf16 = c_ref[...].astype(jnp.bfloat16)
            mod_ref[:, c0:c0 + cols] = _dot(c_bf16, chunk.astype(jnp.bfloat16)) + bada_ref[:, c0:c0 + cols]
        return use

    def fold_cols(c0):
        def use(chunk):
            for g in range(c0 // gw, (c0 + cols) // gw):
                gcols = slice(g * gw, (g + 1) * gw)
                wp = wpool_ref[g] * (0.5 * pscale_ref[:, gcols])
                wfold = jnp.dot(chunk[:, g * gw - c0:(g + 1) * gw - c0], wp, precision=lax.Precision.HIGHEST,
                                preferred_element_type=jnp.float32)
                wfold_ref[:, gcols] = wfold.astype(jnp.bfloat16)
        return use

    def cast_into(dst):
        def use(chunk):
            dst[...] = chunk.astype(jnp.bfloat16)
        return use

    moves = [(wada_hbm.at[:, pl.ds(c0, cols)], ada_cols(c0)) for c0 in range(0, wada_hbm.shape[1], cols)]
    moves += [(win_hbm.at[:, pl.ds(c0, cols)], fold_cols(c0)) for c0 in range(0, d, cols)]
    n_before_step = len(moves)
    later = {}
    for slab in (3, 1, 2, 4, 5):
        later["w_in slab %d" % slab] = list(range(len(moves), len(moves) + d // cols))
        moves += [(win_hbm.at[:, pl.ds(c0, cols)], cast_into(win_ref.at[:, pl.ds(c0 - d, cols)]))
                  for c0 in range(slab * d, (slab + 1) * d, cols)]
    for name, src, dst in (("w_bout", wbout_hbm, wbout_ref), ("w_o", wo_hbm, wo_ref)):
        later[name] = list(range(len(moves), len(moves) + d // cols))
        moves += [(src.at[:, pl.ds(c0, cols)], cast_into(dst.at[:, pl.ds(c0, cols)])) for c0 in range(0, d, cols)]

    def load(i):
        return pltpu.make_async_copy(moves[i][0], stage.at[i % n_slots], stage_sem.at[i % n_slots])

    def consume(i):
        if i + n_slots - 1 < len(moves):
            load(i + n_slots - 1).start()
        load(i).wait()
        moves[i][1](stage[i % n_slots])


    def step(first):
        order = list(later)
        fetched = []

        def need(name):
            if first:
                upto = min(order.index(name) + 1, len(order) - 1)
                for nxt in order[len(fetched):upto + 1]:
                    for i in later[nxt]:
                        consume(i)
                    fetched.append(nxt)

        wup_bf16[...] = wup_f32[...].astype(jnp.bfloat16)
        wdown_bf16[...] = wdown_f32[...].astype(jnp.bfloat16)

        @pl.when(s == 0)
        def _():
            _zero_history(pool_buf, POOL_HALO)
            _zero_history(conv_buf, CONV_HALO)

        x = x_ref[0]
        sh1, sc1, gt1 = (mod_ref[pl.ds(b, 1), i * d:(i + 1) * d] for i in range(3))
        halves = [slice(i * tile // ROW_SPLIT, (i + 1) * tile // ROW_SPLIT) for i in range(ROW_SPLIT)]
        pre_gain = gpre_ref[...] * (1.0 + sc1)
        hb_parts = [(_rmsnorm(x[rows], pre_gain) + sh1).astype(jnp.bfloat16) for rows in halves]
        hb = jnp.concatenate(hb_parts, axis=0)

        def proj(i):
            need("w_in slab %d" % i)
            return _dot(hb, win_ref[:, (i - 1) * d:i * d])

        u = jnp.concatenate([_dot(part, wfold_ref[...]) for part in hb_parts], axis=0)
        _put_slabs(pool_buf, POOL_HALO, u, 0)
        uc = proj(3)
        v = uc * proj(1)
        _put_slabs(conv_buf, CONV_HALO, v, 0)

        t1 = (s * tile + lax.broadcasted_iota(jnp.int32, (tile, V7X_LANES), 0) + 1).astype(jnp.float32)
        half_ya = []
        for g, w in enumerate(POOL_WINDOWS):
            slabs = range(g * gw // V7X_LANES, (g + 1) * gw // V7X_LANES)
            ug = u[:, g * gw:(g + 1) * gw]
            if w >= V7X_SUBLANES:
                hw, ext = w // 2, V7X_SUBLANES
                part = _rows_back(pool_buf, POOL_HALO - ext, tile + ext, slabs, 0)
                for k in range(1, hw):
                    part = part + _rows_back(pool_buf, POOL_HALO - ext, tile + ext, slabs, k)
                acc = part[ext:] + part[ext - hw:ext - hw + tile]
            else:
                acc = ug
                for k in range(1, w):
                    acc = acc + _rows_back(pool_buf, POOL_HALO, tile, slabs, k)
            inv = 1.0 / jnp.minimum(t1, float(w))
            inv = jnp.concatenate([inv] * len(slabs), axis=1)
            half_ya.append(acc * inv - ug)
        _keep_history(pool_buf, POOL_HALO, tile, range(n_slabs))
        half_ya = jnp.concatenate(half_ya, axis=1)

        ub = proj(2)

        v1 = _rows_back(conv_buf, CONV_HALO, tile, range(n_slabs), 1)
        v2 = _rows_back(conv_buf, CONV_HALO, tile, range(n_slabs), 2)
        _keep_history(conv_buf, CONV_HALO, tile, range(n_slabs))
        cw, cb = 0.5 * convw_ref[0], 0.5 * convb_ref[...]
        half_conv = cb + cw[0:1] * v2 + cw[1:2] * v1 + cw[2:3] * v
        half_ubc = (ub * half_conv).astype(jnp.bfloat16)

        za = proj(4)
        zb = proj(5)
        gated_a = half_ya + half_ya * jnp.tanh(0.5 * za)
        need("w_bout")
        half_yb = _dot(half_ubc, wbout_ref[...])
        merged = (gated_a + (half_yb + half_yb * jnp.tanh(0.5 * zb))).astype(jnp.bfloat16)
        post_gain = gpost_ref[...] * gt1
        need("w_o")
        for rows in halves:
            o_ref[0, rows, :] = x[rows] + _rmsnorm(_dot(merged[rows], wo_ref[...]), post_gain)

    @pl.when(is_first)
    def _():
        for i in range(n_slots - 1):
            load(i).start()
        for i in range(n_before_step):
            consume(i)
        step(first=True)

    @pl.when(jnp.logical_not(is_first))
    def _():
        step(first=False)


def _mixer(x, c, w_ada, b_ada, g_pre, g_post, w_in, w_pool, pool_scale, conv_w, layer, conv_b, w_bout, w_o, w_up,
           w_down):
    bsz, seq, d = x.shape
    tile = MIXER_TILE
    n_seq = seq // tile
    n_mod = w_ada.shape[1]
    d_rest = w_in.shape[1] - d
    gw = w_pool.shape[1]
    assert d_rest % WEIGHT_LOAD_COLS == 0 and d % WEIGHT_LOAD_COLS == 0 and n_mod % WEIGHT_LOAD_COLS == 0
    assert WEIGHT_LOAD_COLS % gw == 0 and w_ada.shape[0] == d
    weights = (2 * (d * d + d * d_rest + w_bout.size + w_o.size) + WEIGHT_LOAD_SLOTS * d * WEIGHT_LOAD_COLS * 4
               + w_pool.size * 4 + 2 * bsz * n_mod * 4)
    tok = lambda b, s: (b, s, 0)
    hbm = pl.BlockSpec(memory_space=pl.ANY)

    def side_spec(w):
        n_blocks, rows = _row_blocks(w.shape[0], bsz * n_seq)
        return pl.BlockSpec((rows, w.shape[1]), lambda b, s: (jnp.minimum(b * n_seq + s, n_blocks - 1), 0)), rows

    (up_spec, up_rows), (down_spec, down_rows) = side_spec(w_up), side_spec(w_down)
    side_bytes = (up_rows * w_up.shape[1] + down_rows * w_down.shape[1]) * (4 + 2)
    hist_bytes = d * (POOL_HALO + CONV_HALO + 2 * tile) * 4
    return pl.pallas_call(
        functools.partial(_mixer_kernel, tile=tile, d=d),
        out_shape=(jax.ShapeDtypeStruct(x.shape, x.dtype), jax.ShapeDtypeStruct((bsz, n_mod), jnp.float32),
                   jax.ShapeDtypeStruct(w_up.shape, jnp.bfloat16), jax.ShapeDtypeStruct(w_down.shape, jnp.bfloat16)),
        grid=(bsz, n_seq),
        in_specs=[
            pl.BlockSpec((1, tile, d), tok),
            _const_spec((1, d)), _const_spec((1, d)),
            _layer_spec(conv_w, layer), _const_spec((1, d)),
            _const_spec(c.shape), _const_spec((1, n_mod)), _const_spec(w_pool.shape), _const_spec((1, d)),
            up_spec, down_spec,
            hbm, hbm, hbm, hbm,
        ],
        out_specs=(pl.BlockSpec((1, tile, d), tok), pl.BlockSpec((bsz, n_mod), lambda b, s: (0, 0)),
                   up_spec, down_spec),
        scratch_shapes=[pltpu.VMEM((d // V7X_LANES, POOL_HALO + tile, V7X_LANES), jnp.float32),
                        pltpu.VMEM((d // V7X_LANES, CONV_HALO + tile, V7X_LANES), jnp.float32),
                        pltpu.VMEM((d, d), jnp.bfloat16),
                        pltpu.VMEM((d, d_rest), jnp.bfloat16), pltpu.VMEM(w_bout.shape, jnp.bfloat16),
                        pltpu.VMEM(w_o.shape, jnp.bfloat16),
                        pltpu.VMEM((WEIGHT_LOAD_SLOTS, d, WEIGHT_LOAD_COLS), jnp.float32),
                        pltpu.SemaphoreType.DMA((WEIGHT_LOAD_SLOTS,))],
        compiler_params=pltpu.CompilerParams(
            dimension_semantics=("arbitrary", "arbitrary"),
            vmem_limit_bytes=_vmem_limit(weights + hist_bytes, 2 * tile * d * 4 + side_bytes, 8 * tile * d * 4)),
        name="mixer",
    )(x, g_pre.reshape(1, d), g_post.reshape(1, d), conv_w, conv_b.reshape(1, d),
      c, b_ada.reshape(1, n_mod), w_pool, pool_scale.reshape(1, d), w_up, w_down, w_ada, w_in, w_bout, w_o)


def _ffn_kernel(x_ref, mod_ref, gpre_ref, gpost_ref, wup_ref, cw_ref, cb_ref, wdown_ref, o_ref, up_buf,
                *, tile, d, f, chunks):
    s = pl.program_id(1)

    @pl.when(s == 0)
    def _():
        _zero_history(up_buf, CONV_HALO)

    x = x_ref[0]
    sh2, sc2, gt2 = (mod_ref[pl.ds(pl.program_id(0), 1), i * d:(i + 1) * d] for i in range(3, N_MOD))
    hb = (_rmsnorm(x, gpre_ref[...] * (1.0 + sc2)) + sh2).astype(jnp.bfloat16)

    def conv_cols(lo, n, scale=None):
        cols = slice(lo, lo + n)
        up = _dot(hb, wup_ref[:, cols])
        slabs = range(lo // V7X_LANES, (lo + n) // V7X_LANES)
        _put_slabs(up_buf, CONV_HALO, up, slabs[0])
        u1 = _rows_back(up_buf, CONV_HALO, tile, slabs, 1)
        u2 = _rows_back(up_buf, CONV_HALO, tile, slabs, 2)
        _keep_history(up_buf, CONV_HALO, tile, slabs)
        cw, cb = cw_ref[0, :, cols], cb_ref[:, cols]
        if scale is not None:
            cw, cb = cw * scale, cb * scale
        return cb + cw[0:1] * u2 + cw[1:2] * u1 + cw[2:3] * up

    ff = None
    for lo, n in chunks:
        gate = conv_cols(lo, n)
        half_val = conv_cols(f + lo, n, scale=0.5)
        p = gate * half_val
        t = jnp.tanh(gate * (GELU_K + (GELU_K * GELU_A) * (gate * gate)))
        act = (p + p * t).astype(jnp.bfloat16)
        part = _dot(act, wdown_ref[lo:lo + n, :])
        ff = part if ff is None else ff + part
    o_ref[0] = x + _rmsnorm(ff, gpost_ref[...] * gt2)


def _ffn(x, mod, g_pre, g_post, w_up, conv_w, layer, conv_b, w_down):
    bsz, seq, d = x.shape
    tile = FFN_TILE
    f = w_down.shape[0]
    assert sum(FFN_CHUNKS) == f
    chunks = tuple((sum(FFN_CHUNKS[:i]), n) for i, n in enumerate(FFN_CHUNKS))
    weights = 2 * (w_up.size + w_down.size)
    up_buf_bytes = 2 * f * (CONV_HALO + tile) * 4
    tok = lambda b, s: (b, s, 0)
    return pl.pallas_call(
        functools.partial(_ffn_kernel, tile=tile, d=d, f=f, chunks=chunks),
        out_shape=jax.ShapeDtypeStruct(x.shape, x.dtype),
        grid=(bsz, seq // tile),
        in_specs=[
            pl.BlockSpec((1, tile, d), tok),
            _const_spec(mod.shape),
            _const_spec((1, d)), _const_spec((1, d)),
            _const_spec(w_up.shape), _layer_spec(conv_w, layer), _const_spec((1, 2 * f)),
            _const_spec(w_down.shape),
        ],
        out_specs=pl.BlockSpec((1, tile, d), tok),
        scratch_shapes=[pltpu.VMEM((2 * f // V7X_LANES, CONV_HALO + tile, V7X_LANES), jnp.float32)],
        compiler_params=pltpu.CompilerParams(
            dimension_semantics=("arbitrary", "arbitrary"),
            vmem_limit_bytes=max(_vmem_limit(weights + up_buf_bytes, 2 * tile * d * 4, 4 * tile * d * 4),
                                 V7X_SCOPED_VMEM_MAX_BYTES)),
        name="convffn",
    )(x, mod, g_pre.reshape(1, d), g_post.reshape(1, d), w_up, conv_w, conv_b.reshape(1, 2 * f), w_down)


def kernel(x, c, g_pre_mix, g_post_mix, g_pre_ffn, g_post_ffn, w_ada, b_ada, w_in, w_pool, pool_scale, conv_w,
           conv_b, w_bout, w_o, w_up, ffn_conv_w, ffn_conv_b, w_down):
    for l in range(w_ada.shape[0]):
        x, mod, w_up_bf16, w_down_bf16 = _mixer(x, c, w_ada[l], b_ada[l], g_pre_mix[l], g_post_mix[l], w_in[l],
                                                w_pool[l], pool_scale[l], conv_w, l, conv_b[l], w_bout[l], w_o[l],
                                                w_up[l], w_down[l])
        x = _ffn(x, mod, g_pre_ffn[l], g_post_ffn[l], w_up_bf16, ffn_conv_w, l, ffn_conv_b[l], w_down_bf16)
    return x
```

```python
import functools

import jax
import jax.numpy as jnp
from jax import lax
from jax.experimental import pallas as pl
from jax.experimental.pallas import tpu as pltpu

POOL_WINDOWS = (2, 4, 8, 16)
N_MOD = 6
EPS = 1e-6
GELU_K = 0.7978845608028654
GELU_A = 0.044715

V7X_LANES = 128
V7X_SUBLANES = 8
V7X_BF16_SUBLANES = 16
V7X_VMEM_BYTES = 64 * 1024 * 1024
V7X_SCOPED_VMEM_MAX_BYTES = 60000 * 1024

POOL_HALO = 16
CONV_HALO = 8

MIXER_TILE = 512
FFN_TILE = 512
WEIGHT_LOAD_COLS = 512
WEIGHT_LOAD_SLOTS = 4
ROW_SPLIT = 2
FFN_CHUNKS = (1536, 1280)


def _vmem_limit(resident_bytes, streamed_bytes, temp_bytes):
    need = resident_bytes + 2 * streamed_bytes + temp_bytes
    assert need < V7X_VMEM_BYTES, need
    return int(need)


def _const_spec(shape):
    zeros = (0,) * len(shape)
    return pl.BlockSpec(shape, lambda *_: zeros, pipeline_mode=pl.Buffered(1))


def _layer_spec(stacked, layer):
    zeros = (0,) * (stacked.ndim - 1)
    return pl.BlockSpec((1,) + stacked.shape[1:], lambda *_: (layer,) + zeros, pipeline_mode=pl.Buffered(1))


def _rmsnorm(xf, gain_row):
    return xf * lax.rsqrt(jnp.mean(xf * xf, axis=-1, keepdims=True) + EPS) * gain_row


def _put_slabs(buf, halo, val, first):
    for i in range(val.shape[1] // V7X_LANES):
        buf[first + i, halo:, :] = val[:, i * V7X_LANES:(i + 1) * V7X_LANES]


def _rows_back(buf, halo, tile, slabs, k):
    return jnp.concatenate([buf[j, halo - k:halo - k + tile, :] for j in slabs], axis=1)


def _keep_history(buf, halo, tile, slabs):
    for j in slabs:
        buf[j, :halo, :] = buf[j, tile:, :]


def _zero_history(buf, halo):
    buf[:, :halo, :] = jnp.zeros((buf.shape[0], halo, V7X_LANES), jnp.float32)


def _dot(a, b):
    return jnp.dot(a, b, preferred_element_type=jnp.float32)


def _row_blocks(rows, steps):
    for n in range(steps, 0, -1):
        if rows % n == 0 and (rows // n) % V7X_BF16_SUBLANES == 0:
            return n, rows // n
    raise ValueError((rows, steps))


def _mixer_kernel(x_ref, gpre_ref, gpost_ref, convw_ref, convb_ref, c_ref, bada_ref, wpool_ref, pscale_ref,
                  wup_f32, wdown_f32, wada_hbm, win_hbm, wbout_hbm, wo_hbm,
                  o_ref, mod_ref, wup_bf16, wdown_bf16,
                  pool_buf, conv_buf, wfold_ref, win_ref, wbout_ref, wo_ref, stage, stage_sem, *, tile, d):
    b, s = pl.program_id(0), pl.program_id(1)
    n_slabs = d // V7X_LANES
    is_first = (b == 0) & (s == 0)
    cols, n_slots = stage.shape[2], stage.shape[0]
    gw = wpool_ref.shape[1]


    def ada_cols(c0):
        def use(chunk):
            c_bf16 = c_ref[...].astype(jnp.bfloat16)
            mod_ref[:, c0:c0 + cols] = _dot(c_bf16, chunk.astype(jnp.bfloat16)) + bada_ref[:, c0:c0 + cols]
        return use

    def fold_cols(c0):
        def use(chunk):
            for g in range(c0 // gw, (c0 + cols) // gw):
                gcols = slice(g * gw, (g + 1) * gw)
                wp = wpool_ref[g] * (0.5 * pscale_ref[:, gcols])
                wfold = jnp.dot(chunk[:, g * gw - c0:(g + 1) * gw - c0], wp, precision=lax.Precision.HIGHEST,
                                preferred_element_type=jnp.float32)
                wfold_ref[:, gcols] = wfold.astype(jnp.bfloat16)
        return use

    def cast_into(dst, scale=None):
        def use(chunk):
            dst[...] = (chunk if scale is None else chunk * scale).astype(jnp.bfloat16)
        return use

    moves = [(wada_hbm.at[:, pl.ds(c0, cols)], ada_cols(c0)) for c0 in range(0, wada_hbm.shape[1], cols)]
    moves += [(win_hbm.at[:, pl.ds(c0, cols)], fold_cols(c0)) for c0 in range(0, d, cols)]
    n_before_step = len(moves)
    later = {}
    for slab in (3, 1, 2, 4, 5):
        later["w_in slab %d" % slab] = list(range(len(moves), len(moves) + d // cols))
        moves += [(win_hbm.at[:, pl.ds(c0, cols)],
                   cast_into(win_ref.at[:, pl.ds(c0 - d, cols)], 0.5 if slab >= 4 else None))
                  for c0 in range(slab * d, (slab + 1) * d, cols)]
    for name, src, dst in (("w_bout", wbout_hbm, wbout_ref), ("w_o", wo_hbm, wo_ref)):
        later[name] = list(range(len(moves), len(moves) + d // cols))
        moves += [(src.at[:, pl.ds(c0, cols)], cast_into(dst.at[:, pl.ds(c0, cols)])) for c0 in range(0, d, cols)]

    def load(i):
        return pltpu.make_async_copy(moves[i][0], stage.at[i % n_slots], stage_sem.at[i % n_slots])

    def consume(i):
        if i + n_slots - 1 < len(moves):
            load(i + n_slots - 1).start()
        load(i).wait()
        moves[i][1](stage[i % n_slots])


    def step(first):
        order = list(later)
        fetched = []

        def need(name):
            if first:
                upto = min(order.index(name) + 1, len(order) - 1)
                for nxt in order[len(fetched):upto + 1]:
                    for i in later[nxt]:
                        consume(i)
                    fetched.append(nxt)

        wup_bf16[...] = wup_f32[...].astype(jnp.bfloat16)
        wdown_bf16[...] = wdown_f32[...].astype(jnp.bfloat16)

        @pl.when(s == 0)
        def _():
            _zero_history(pool_buf, POOL_HALO)
            _zero_history(conv_buf, CONV_HALO)

        x = x_ref[0]
        sh1, sc1, gt1 = (mod_ref[pl.ds(b, 1), i * d:(i + 1) * d] for i in range(3))
        halves = [slice(i * tile // ROW_SPLIT, (i + 1) * tile // ROW_SPLIT) for i in range(ROW_SPLIT)]
        pre_gain = gpre_ref[...] * (1.0 + sc1)
        hb_parts = [(_rmsnorm(x[rows], pre_gain) + sh1).astype(jnp.bfloat16) for rows in halves]
        hb = jnp.concatenate(hb_parts, axis=0)

        def proj(i):
            need("w_in slab %d" % i)
            return _dot(hb, win_ref[:, (i - 1) * d:i * d])

        u = jnp.concatenate([_dot(part, wfold_ref[...]) for part in hb_parts], axis=0)
        _put_slabs(pool_buf, POOL_HALO, u, 0)
        uc = proj(3)
        v = uc * proj(1)
        _put_slabs(conv_buf, CONV_HALO, v, 0)

        t1 = (s * tile + lax.broadcasted_iota(jnp.int32, (tile, V7X_LANES), 0) + 1).astype(jnp.float32)
        half_ya = []
        for g, w in enumerate(POOL_WINDOWS):
            slabs = range(g * gw // V7X_LANES, (g + 1) * gw // V7X_LANES)
            ug = u[:, g * gw:(g + 1) * gw]
            if w >= V7X_SUBLANES:
                hw, ext = w // 2, V7X_SUBLANES
                part = _rows_back(pool_buf, POOL_HALO - ext, tile + ext, slabs, 0)
                for k in range(1, hw):
                    part = part + _rows_back(pool_buf, POOL_HALO - ext, tile + ext, slabs, k)
                acc = part[ext:] + part[ext - hw:ext - hw + tile]
            else:
                acc = ug
                for k in range(1, w):
                    acc = acc + _rows_back(pool_buf, POOL_HALO, tile, slabs, k)
            inv = 1.0 / jnp.minimum(t1, float(w))
            inv = jnp.concatenate([inv] * len(slabs), axis=1)
            half_ya.append(acc * inv - ug)
        _keep_history(pool_buf, POOL_HALO, tile, range(n_slabs))
        half_ya = jnp.concatenate(half_ya, axis=1)

        ub = proj(2)

        v1 = _rows_back(conv_buf, CONV_HALO, tile, range(n_slabs), 1)
        v2 = _rows_back(conv_buf, CONV_HALO, tile, range(n_slabs), 2)
        _keep_history(conv_buf, CONV_HALO, tile, range(n_slabs))
        cw, cb = 0.5 * convw_ref[0], 0.5 * convb_ref[...]
        half_conv = cb + cw[0:1] * v2 + cw[1:2] * v1 + cw[2:3] * v
        half_ubc = (ub * half_conv).astype(jnp.bfloat16)

        half_za = proj(4)
        half_zb = proj(5)
        gated_a = half_ya + half_ya * jnp.tanh(half_za)
        need("w_bout")
        half_yb = _dot(half_ubc, wbout_ref[...])
        merged = (gated_a + (half_yb + half_yb * jnp.tanh(half_zb))).astype(jnp.bfloat16)
        post_gain = gpost_ref[...] * gt1
        need("w_o")
        for rows in halves:
            o_ref[0, rows, :] = x[rows] + _rmsnorm(_dot(merged[rows], wo_ref[...]), post_gain)

    @pl.when(is_first)
    def _():
        for i in range(n_slots - 1):
            load(i).start()
        for i in range(n_before_step):
            consume(i)
        step(first=True)

    @pl.when(jnp.logical_not(is_first))
    def _():
        step(first=False)


def _mixer(x, c, w_ada, b_ada, g_pre, g_post, w_in, w_pool, pool_scale, conv_w, layer, conv_b, w_bout, w_o, w_up,
           w_down):
    bsz, seq, d = x.shape
    tile = MIXER_TILE
    n_seq = seq // tile
    n_mod = w_ada.shape[1]
    d_rest = w_in.shape[1] - d
    gw = w_pool.shape[1]
    assert d_rest % WEIGHT_LOAD_COLS == 0 and d % WEIGHT_LOAD_COLS == 0 and n_mod % WEIGHT_LOAD_COLS == 0
    assert WEIGHT_LOAD_COLS % gw == 0 and w_ada.shape[0] == d
    weights = (2 * (d * d + d * d_rest + w_bout.size + w_o.size) + WEIGHT_LOAD_SLOTS * d * WEIGHT_LOAD_COLS * 4
               + w_pool.size * 4 + 2 * bsz * n_mod * 4)
    tok = lambda b, s: (b, s, 0)
    hbm = pl.BlockSpec(memory_space=pl.ANY)

    def side_spec(w):
        n_blocks, rows = _row_blocks(w.shape[0], bsz * n_seq)
        return pl.BlockSpec((rows, w.shape[1]), lambda b, s: (jnp.minimum(b * n_seq + s, n_blocks - 1), 0)), rows

    (up_spec, up_rows), (down_spec, down_rows) = side_spec(w_up), side_spec(w_down)
    side_bytes = (up_rows * w_up.shape[1] + down_rows * w_down.shape[1]) * (4 + 2)
    hist_bytes = d * (POOL_HALO + CONV_HALO + 2 * tile) * 4
    return pl.pallas_call(
        functools.partial(_mixer_kernel, tile=tile, d=d),
        out_shape=(jax.ShapeDtypeStruct(x.shape, x.dtype), jax.ShapeDtypeStruct((bsz, n_mod), jnp.float32),
                   jax.ShapeDtypeStruct(w_up.shape, jnp.bfloat16), jax.ShapeDtypeStruct(w_down.shape, jnp.bfloat16)),
        grid=(bsz, n_seq),
        in_specs=[
            pl.BlockSpec((1, tile, d), tok),
            _const_spec((1, d)), _const_spec((1, d)),
            _layer_spec(conv_w, layer), _const_spec((1, d)),
            _const_spec(c.shape), _const_spec((1, n_mod)), _const_spec(w_pool.shape), _const_spec((1, d)),
            up_spec, down_spec,
            hbm, hbm, hbm, hbm,
        ],
        out_specs=(pl.BlockSpec((1, tile, d), tok), pl.BlockSpec((bsz, n_mod), lambda b, s: (0, 0)),
                   up_spec, down_spec),
        scratch_shapes=[pltpu.VMEM((d // V7X_LANES, POOL_HALO + tile, V7X_LANES), jnp.float32),
                        pltpu.VMEM((d // V7X_LANES, CONV_HALO + tile, V7X_LANES), jnp.float32),
                        pltpu.VMEM((d, d), jnp.bfloat16),
                        pltpu.VMEM((d, d_rest), jnp.bfloat16), pltpu.VMEM(w_bout.shape, jnp.bfloat16),
                        pltpu.VMEM(w_o.shape, jnp.bfloat16),
                        pltpu.VMEM((WEIGHT_LOAD_SLOTS, d, WEIGHT_LOAD_COLS), jnp.float32),
                        pltpu.SemaphoreType.DMA((WEIGHT_LOAD_SLOTS,))],
        compiler_params=pltpu.CompilerParams(
            dimension_semantics=("arbitrary", "arbitrary"),
            vmem_limit_bytes=_vmem_limit(weights + hist_bytes, 2 * tile * d * 4 + side_bytes, 8 * tile * d * 4)),
        name="mixer",
    )(x, g_pre.reshape(1, d), g_post.reshape(1, d), conv_w, conv_b.reshape(1, d),
      c, b_ada.reshape(1, n_mod), w_pool, pool_scale.reshape(1, d), w_up, w_down, w_ada, w_in, w_bout, w_o)


def _ffn_kernel(x_ref, mod_ref, gpre_ref, gpost_ref, wup_ref, cw_ref, cb_ref, wdown_ref, o_ref, up_buf,
                *, tile, d, f, chunks):
    s = pl.program_id(1)

    @pl.when(s == 0)
    def _():
        _zero_history(up_buf, CONV_HALO)

    x = x_ref[0]
    sh2, sc2, gt2 = (mod_ref[pl.ds(pl.program_id(0), 1), i * d:(i + 1) * d] for i in range(3, N_MOD))
    hb = (_rmsnorm(x, gpre_ref[...] * (1.0 + sc2)) + sh2).astype(jnp.bfloat16)

    def conv_cols(lo, n, scale=None):
        cols = slice(lo, lo + n)
        up = _dot(hb, wup_ref[:, cols])
        slabs = range(lo // V7X_LANES, (lo + n) // V7X_LANES)
        _put_slabs(up_buf, CONV_HALO, up, slabs[0])
        u1 = _rows_back(up_buf, CONV_HALO, tile, slabs, 1)
        u2 = _rows_back(up_buf, CONV_HALO, tile, slabs, 2)
        _keep_history(up_buf, CONV_HALO, tile, slabs)
        cw, cb = cw_ref[0, :, cols], cb_ref[:, cols]
        if scale is not None:
            cw, cb = cw * scale, cb * scale
        return cb + cw[0:1] * u2 + cw[1:2] * u1 + cw[2:3] * up

    ff = None
    for lo, n in chunks:
        gate = conv_cols(lo, n)
        half_val = conv_cols(f + lo, n, scale=0.5)
        p = gate * half_val
        t = jnp.tanh(gate * (GELU_K + (GELU_K * GELU_A) * (gate * gate)))
        act = (p + p * t).astype(jnp.bfloat16)
        part = _dot(act, wdown_ref[lo:lo + n, :])
        ff = part if ff is None else ff + part
    o_ref[0] = x + _rmsnorm(ff, gpost_ref[...] * gt2)


def _ffn(x, mod, g_pre, g_post, w_up, conv_w, layer, conv_b, w_down):
    bsz, seq, d = x.shape
    tile = FFN_TILE
    f = w_down.shape[0]
    assert sum(FFN_CHUNKS) == f
    chunks = tuple((sum(FFN_CHUNKS[:i]), n) for i, n in enumerate(FFN_CHUNKS))
    weights = 2 * (w_up.size + w_down.size)
    up_buf_bytes = 2 * f * (CONV_HALO + tile) * 4
    tok = lambda b, s: (b, s, 0)
    return pl.pallas_call(
        functools.partial(_ffn_kernel, tile=tile, d=d, f=f, chunks=chunks),
        out_shape=jax.ShapeDtypeStruct(x.shape, x.dtype),
        grid=(bsz, seq // tile),
        in_specs=[
            pl.BlockSpec((1, tile, d), tok),
            _const_spec(mod.shape),
            _const_spec((1, d)), _const_spec((1, d)),
            _const_spec(w_up.shape), _layer_spec(conv_w, layer), _const_spec((1, 2 * f)),
            _const_spec(w_down.shape),
        ],
        out_specs=pl.BlockSpec((1, tile, d), tok),
        scratch_shapes=[pltpu.VMEM((2 * f // V7X_LANES, CONV_HALO + tile, V7X_LANES), jnp.float32)],
        compiler_params=pltpu.CompilerParams(
            dimension_semantics=("arbitrary", "arbitrary"),
            vmem_limit_bytes=max(_vmem_limit(weights + up_buf_bytes, 2 * tile * d * 4, 4 * tile * d * 4),
                                 V7X_SCOPED_VMEM_MAX_BYTES)),
        name="convffn",
    )(x, mod, g_pre.reshape(1, d), g_post.reshape(1, d), w_up, conv_w, conv_b.reshape(1, 2 * f), w_down)


def kernel(x, c, g_pre_mix, g_post_mix, g_pre_ffn, g_post_ffn, w_ada, b_ada, w_in, w_pool, pool_scale, conv_w,
           conv_b, w_bout, w_o, w_up, ffn_conv_w, ffn_conv_b, w_down):
    for l in range(w_ada.shape[0]):
        x, mod, w_up_bf16, w_down_bf16 = _mixer(x, c, w_ada[l], b_ada[l], g_pre_mix[l], g_post_mix[l], w_in[l],
                                                w_pool[l], pool_scale[l], conv_w, l, conv_b[l], w_bout[l], w_o[l],
                                                w_up[l], w_down[l])
        x = _ffn(x, mod, g_pre_ffn[l], g_post_ffn[l], w_up_bf16, ffn_conv_w, l, ffn_conv_b[l], w_down_bf16)
    return x
```

```python
import functools

import jax
import jax.numpy as jnp
from jax import lax
from jax.experimental import pallas as pl
from jax.experimental.pallas import tpu as pltpu

POOL_WINDOWS = (2, 4, 8, 16)
N_MOD = 6
EPS = 1e-6
GELU_K = 0.7978845608028654
GELU_A = 0.044715

V7X_LANES = 128
V7X_SUBLANES = 8
V7X_BF16_SUBLANES = 16
V7X_VMEM_BYTES = 64 * 1024 * 1024
V7X_SCOPED_VMEM_MAX_BYTES = 60000 * 1024

POOL_HALO = 16
CONV_HALO = 8

MIXER_TILE = 512
FFN_TILE = 512
WEIGHT_LOAD_COLS = 512
WEIGHT_LOAD_SLOTS = 4
ROW_SPLIT = 2
FFN_CHUNKS = (1280, 1536)


def _vmem_limit(resident_bytes, streamed_bytes, temp_bytes):
    need = resident_bytes + 2 * streamed_bytes + temp_bytes
    assert need < V7X_VMEM_BYTES, need
    return int(need)


def _const_spec(shape):
    zeros = (0,) * len(shape)
    return pl.BlockSpec(shape, lambda *_: zeros, pipeline_mode=pl.Buffered(1))


def _layer_spec(stacked, layer):
    zeros = (0,) * (stacked.ndim - 1)
    return pl.BlockSpec((1,) + stacked.shape[1:], lambda *_: (layer,) + zeros, pipeline_mode=pl.Buffered(1))


def _rmsnorm(xf, gain_row):
    return xf * lax.rsqrt(jnp.mean(xf * xf, axis=-1, keepdims=True) + EPS) * gain_row


def _put_slabs(buf, halo, val, first):
    for i in range(val.shape[1] // V7X_LANES):
        buf[first + i, halo:, :] = val[:, i * V7X_LANES:(i + 1) * V7X_LANES]


def _rows_back(buf, halo, tile, slabs, k):
    return jnp.concatenate([buf[j, halo - k:halo - k + tile, :] for j in slabs], axis=1)


def _keep_history(buf, halo, tile, slabs):
    for j in slabs:
        buf[j, :halo, :] = buf[j, tile:, :]


def _zero_history(buf, halo):
    buf[:, :halo, :] = jnp.zeros((buf.shape[0], halo, V7X_LANES), jnp.float32)


def _dot(a, b):
    return jnp.dot(a, b, preferred_element_type=jnp.float32)


def _row_blocks(rows, steps):
    for n in range(steps, 0, -1):
        if rows % n == 0 and (rows // n) % V7X_BF16_SUBLANES == 0:
            return n, rows // n
    raise ValueError((rows, steps))


def _mixer_kernel(x_ref, gpre_ref, gpost_ref, convw_ref, convb_ref, c_ref, bada_ref, wpool_ref, pscale_ref,
                  wup_f32, wdown_f32, wada_hbm, win_hbm, wbout_hbm, wo_hbm,
                  o_ref, mod_ref, wup_bf16, wdown_bf16,
                  pool_buf, conv_buf, wfold_ref, win_ref, wbout_ref, wo_ref, stage, stage_sem, *, tile, d):
    b, s = pl.program_id(0), pl.program_id(1)
    n_slabs = d // V7X_LANES
    is_first = (b == 0) & (s == 0)
    cols, n_slots = stage.shape[2], stage.shape[0]
    gw = wpool_ref.shape[1]


    def ada_cols(c0):
        def use(chunk):
            c_bf16 = c_ref[...].astype(jnp.bfloat16)
            mod_ref[:, c0:c0 + cols] = _dot(c_bf16, chunk.astype(jnp.bfloat16)) + bada_ref[:, c0:c0 + cols]
        return use

    def fold_cols(c0):
        def use(chunk):
            for g in range(c0 // gw, (c0 + cols) // gw):
                gcols = slice(g * gw, (g + 1) * gw)
                wp = wpool_ref[g] * (0.5 * pscale_ref[:, gcols])
                wfold = jnp.dot(chunk[:, g * gw - c0:(g + 1) * gw - c0], wp, precision=lax.Precision.HIGHEST,
                                preferred_element_type=jnp.float32)
                wfold_ref[:, gcols] = wfold.astype(jnp.bfloat16)
        return use

    def cast_into(dst, scale=None):
        def use(chunk):
            dst[...] = (chunk if scale is None else chunk * scale).astype(jnp.bfloat16)
        return use

    moves = [(wada_hbm.at[:, pl.ds(c0, cols)], ada_cols(c0)) for c0 in range(0, wada_hbm.shape[1], cols)]
    moves += [(win_hbm.at[:, pl.ds(c0, cols)], fold_cols(c0)) for c0 in range(0, d, cols)]
    n_before_step = len(moves)
    later = {}
    for slab in (3, 1, 2, 4, 5):
        later["w_in slab %d" % slab] = list(range(len(moves), len(moves) + d // cols))
        moves += [(win_hbm.at[:, pl.ds(c0, cols)],
                   cast_into(win_ref.at[:, pl.ds(c0 - d, cols)], 0.5 if slab >= 4 else None))
                  for c0 in range(slab * d, (slab + 1) * d, cols)]
    for name, src, dst in (("w_bout", wbout_hbm, wbout_ref), ("w_o", wo_hbm, wo_ref)):
        later[name] = list(range(len(moves), len(moves) + d // cols))
        moves += [(src.at[:, pl.ds(c0, cols)], cast_into(dst.at[:, pl.ds(c0, cols)])) for c0 in range(0, d, cols)]

    def load(i):
        return pltpu.make_async_copy(moves[i][0], stage.at[i % n_slots], stage_sem.at[i % n_slots])

    def consume(i):
        if i + n_slots - 1 < len(moves):
            load(i + n_slots - 1).start()
        load(i).wait()
        moves[i][1](stage[i % n_slots])


    def step(first):
        order = list(later)
        fetched = []

        def need(name):
            if first:
                upto = min(order.index(name) + 1, len(order) - 1)
                for nxt in order[len(fetched):upto + 1]:
                    for i in later[nxt]:
                        consume(i)
                    fetched.append(nxt)

        wup_bf16[...] = wup_f32[...].astype(jnp.bfloat16)
        wdown_bf16[...] = wdown_f32[...].astype(jnp.bfloat16)

        @pl.when(s == 0)
        def _():
            _zero_history(pool_buf, POOL_HALO)
            _zero_history(conv_buf, CONV_HALO)

        x = x_ref[0]
        sh1, sc1, gt1 = (mod_ref[pl.ds(b, 1), i * d:(i + 1) * d] for i in range(3))
        halves = [slice(i * tile // ROW_SPLIT, (i + 1) * tile // ROW_SPLIT) for i in range(ROW_SPLIT)]
        pre_gain = gpre_ref[...] * (1.0 + sc1)
        hb_parts = [(_rmsnorm(x[rows], pre_gain) + sh1).astype(jnp.bfloat16) for rows in halves]
        hb = jnp.concatenate(hb_parts, axis=0)

        def proj(i):
            need("w_in slab %d" % i)
            return _dot(hb, win_ref[:, (i - 1) * d:i * d])

        u = jnp.concatenate([_dot(part, wfold_ref[...]) for part in hb_parts], axis=0)
        _put_slabs(pool_buf, POOL_HALO, u, 0)
        uc = proj(3)
        v = uc * proj(1)
        _put_slabs(conv_buf, CONV_HALO, v, 0)

        t1 = (s * tile + lax.broadcasted_iota(jnp.int32, (tile, V7X_LANES), 0) + 1).astype(jnp.float32)
        half_ya = []
        for g, w in enumerate(POOL_WINDOWS):
            slabs = range(g * gw // V7X_LANES, (g + 1) * gw // V7X_LANES)
            ug = u[:, g * gw:(g + 1) * gw]
            if w >= V7X_SUBLANES:
                hw, ext = w // 2, V7X_SUBLANES
                part = _rows_back(pool_buf, POOL_HALO - ext, tile + ext, slabs, 0)
                for k in range(1, hw):
                    part = part + _rows_back(pool_buf, POOL_HALO - ext, tile + ext, slabs, k)
                acc = part[ext:] + part[ext - hw:ext - hw + tile]
            else:
                acc = ug
                for k in range(1, w):
                    acc = acc + _rows_back(pool_buf, POOL_HALO, tile, slabs, k)
            inv = 1.0 / jnp.minimum(t1, float(w))
            inv = jnp.concatenate([inv] * len(slabs), axis=1)
            half_ya.append(acc * inv - ug)
        _keep_history(pool_buf, POOL_HALO, tile, range(n_slabs))
        half_ya = jnp.concatenate(half_ya, axis=1)

        ub = proj(2)

        v1 = _rows_back(conv_buf, CONV_HALO, tile, range(n_slabs), 1)
        v2 = _rows_back(conv_buf, CONV_HALO, tile, range(n_slabs), 2)
        _keep_history(conv_buf, CONV_HALO, tile, range(n_slabs))
        cw, cb = 0.5 * convw_ref[0], 0.5 * convb_ref[...]
        half_conv = cb + cw[0:1] * v2 + cw[1:2] * v1 + cw[2:3] * v
        half_ubc = (ub * half_conv).astype(jnp.bfloat16)

        half_za = proj(4)
        half_zb = proj(5)
        gated_a = half_ya + half_ya * jnp.tanh(half_za)
        need("w_bout")
        half_yb = _dot(half_ubc, wbout_ref[...])
        merged = (gated_a + (half_yb + half_yb * jnp.tanh(half_zb))).astype(jnp.bfloat16)
        post_gain = gpost_ref[...] * gt1
        need("w_o")
        for rows in halves:
            o_ref[0, rows, :] = x[rows] + _rmsnorm(_dot(merged[rows], wo_ref[...]), post_gain)

    @pl.when(is_first)
    def _():
        for i in range(n_slots - 1):
            load(i).start()
        for i in range(n_before_step):
            consume(i)
        step(first=True)

    @pl.when(jnp.logical_not(is_first))
    def _():
        step(first=False)


def _mixer(x, c, w_ada, b_ada, g_pre, g_post, w_in, w_pool, pool_scale, conv_w, layer, conv_b, w_bout, w_o, w_up,
           w_down):
    bsz, seq, d = x.shape
    tile = MIXER_TILE
    n_seq = seq // tile
    n_mod = w_ada.shape[1]
    d_rest = w_in.shape[1] - d
    gw = w_pool.shape[1]
    assert d_rest % WEIGHT_LOAD_COLS == 0 and d % WEIGHT_LOAD_COLS == 0 and n_mod % WEIGHT_LOAD_COLS == 0
    assert WEIGHT_LOAD_COLS % gw == 0 and w_ada.shape[0] == d
    weights = (2 * (d * d + d * d_rest + w_bout.size + w_o.size) + WEIGHT_LOAD_SLOTS * d * WEIGHT_LOAD_COLS * 4
               + w_pool.size * 4 + 2 * bsz * n_mod * 4)
    tok = lambda b, s: (b, s, 0)
    hbm = pl.BlockSpec(memory_space=pl.ANY)

    def side_spec(w):
        n_blocks, rows = _row_blocks(w.shape[0], bsz * n_seq)
        return pl.BlockSpec((rows, w.shape[1]), lambda b, s: (jnp.minimum(b * n_seq + s, n_blocks - 1), 0)), rows

    (up_spec, up_rows), (down_spec, down_rows) = side_spec(w_up), side_spec(w_down)
    side_bytes = (up_rows * w_up.shape[1] + down_rows * w_down.shape[1]) * (4 + 2)
    hist_bytes = d * (POOL_HALO + CONV_HALO + 2 * tile) * 4
    return pl.pallas_call(
        functools.partial(_mixer_kernel, tile=tile, d=d),
        out_shape=(jax.ShapeDtypeStruct(x.shape, x.dtype), jax.ShapeDtypeStruct((bsz, n_mod), jnp.float32),
                   jax.ShapeDtypeStruct(w_up.shape, jnp.bfloat16), jax.ShapeDtypeStruct(w_down.shape, jnp.bfloat16)),
        grid=(bsz, n_seq),
        in_specs=[
            pl.BlockSpec((1, tile, d), tok),
            _const_spec((1, d)), _const_spec((1, d)),
            _layer_spec(conv_w, layer), _const_spec((1, d)),
            _const_spec(c.shape), _const_spec((1, n_mod)), _const_spec(w_pool.shape), _const_spec((1, d)),
            up_spec, down_spec,
            hbm, hbm, hbm, hbm,
        ],
        out_specs=(pl.BlockSpec((1, tile, d), tok), pl.BlockSpec((bsz, n_mod), lambda b, s: (0, 0)),
                   up_spec, down_spec),
        scratch_shapes=[pltpu.VMEM((d // V7X_LANES, POOL_HALO + tile, V7X_LANES), jnp.float32),
                        pltpu.VMEM((d // V7X_LANES, CONV_HALO + tile, V7X_LANES), jnp.float32),
                        pltpu.VMEM((d, d), jnp.bfloat16),
                        pltpu.VMEM((d, d_rest), jnp.bfloat16), pltpu.VMEM(w_bout.shape, jnp.bfloat16),
                        pltpu.VMEM(w_o.shape, jnp.bfloat16),
                        pltpu.VMEM((WEIGHT_LOAD_SLOTS, d, WEIGHT_LOAD_COLS), jnp.float32),
                        pltpu.SemaphoreType.DMA((WEIGHT_LOAD_SLOTS,))],
        compiler_params=pltpu.CompilerParams(
            dimension_semantics=("arbitrary", "arbitrary"),
            vmem_limit_bytes=_vmem_limit(weights + hist_bytes, 2 * tile * d * 4 + side_bytes, 8 * tile * d * 4)),
        name="mixer",
    )(x, g_pre.reshape(1, d), g_post.reshape(1, d), conv_w, conv_b.reshape(1, d),
      c, b_ada.reshape(1, n_mod), w_pool, pool_scale.reshape(1, d), w_up, w_down, w_ada, w_in, w_bout, w_o)


def _ffn_kernel(x_ref, mod_ref, gpre_ref, gpost_ref, wup_ref, cw_ref, cb_ref, wdown_ref, o_ref, up_buf,
                *, tile, d, f, chunks):
    s = pl.program_id(1)

    @pl.when(s == 0)
    def _():
        _zero_history(up_buf, CONV_HALO)

    x = x_ref[0]
    sh2, sc2, gt2 = (mod_ref[pl.ds(pl.program_id(0), 1), i * d:(i + 1) * d] for i in range(3, N_MOD))
    hb = (_rmsnorm(x, gpre_ref[...] * (1.0 + sc2)) + sh2).astype(jnp.bfloat16)

    def conv_cols(lo, n, scale=None):
        cols = slice(lo, lo + n)
        up = _dot(hb, wup_ref[:, cols])
        slabs = range(lo // V7X_LANES, (lo + n) // V7X_LANES)
        _put_slabs(up_buf, CONV_HALO, up, slabs[0])
        u1 = _rows_back(up_buf, CONV_HALO, tile, slabs, 1)
        u2 = _rows_back(up_buf, CONV_HALO, tile, slabs, 2)
        _keep_history(up_buf, CONV_HALO, tile, slabs)
        cw, cb = cw_ref[0, :, cols], cb_ref[:, cols]
        if scale is not None:
            cw, cb = cw * scale, cb * scale
        return cb + cw[0:1] * u2 + cw[1:2] * u1 + cw[2:3] * up

    ff = None
    for lo, n in chunks:
        gate = conv_cols(lo, n)
        half_val = conv_cols(f + lo, n, scale=0.5)
        p = gate * half_val
        t = jnp.tanh(gate * (GELU_K + (GELU_K * GELU_A) * (gate * gate)))
        act = (p + p * t).astype(jnp.bfloat16)
        part = _dot(act, wdown_ref[lo:lo + n, :])
        ff = part if ff is None else ff + part
    o_ref[0] = x + _rmsnorm(ff, gpost_ref[...] * gt2)


def _ffn(x, mod, g_pre, g_post, w_up, conv_w, layer, conv_b, w_down):
    bsz, seq, d = x.shape
    tile = FFN_TILE
    f = w_down.shape[0]
    assert sum(FFN_CHUNKS) == f
    chunks = tuple((sum(FFN_CHUNKS[:i]), n) for i, n in enumerate(FFN_CHUNKS))
    weights = 2 * (w_up.size + w_down.size)
    up_buf_bytes = 2 * f * (CONV_HALO + tile) * 4
    tok = lambda b, s: (b, s, 0)
    return pl.pallas_call(
        functools.partial(_ffn_kernel, tile=tile, d=d, f=f, chunks=chunks),
        out_shape=jax.ShapeDtypeStruct(x.shape, x.dtype),
        grid=(bsz, seq // tile),
        in_specs=[
            pl.BlockSpec((1, tile, d), tok),
            _const_spec(mod.shape),
            _const_spec((1, d)), _const_spec((1, d)),
            _const_spec(w_up.shape), _layer_spec(conv_w, layer), _const_spec((1, 2 * f)),
            _const_spec(w_down.shape),
        ],
        out_specs=pl.BlockSpec((1, tile, d), tok),
        scratch_shapes=[pltpu.VMEM((2 * f // V7X_LANES, CONV_HALO + tile, V7X_LANES), jnp.float32)],
        compiler_params=pltpu.CompilerParams(
            dimension_semantics=("arbitrary", "arbitrary"),
            vmem_limit_bytes=max(_vmem_limit(weights + up_buf_bytes, 2 * tile * d * 4, 4 * tile * d * 4),
                                 V7X_SCOPED_VMEM_MAX_BYTES)),
        name="convffn",
    )(x, mod, g_pre.reshape(1, d), g_post.reshape(1, d), w_up, conv_w, conv_b.reshape(1, 2 * f), w_down)


def kernel(x, c, g_pre_mix, g_post_mix, g_pre_ffn, g_post_ffn, w_ada, b_ada, w_in, w_pool, pool_scale, conv_w,
           conv_b, w_bout, w_o, w_up, ffn_conv_w, ffn_conv_b, w_down):
    for l in range(w_ada.shape[0]):
        x, mod, w_up_bf16, w_down_bf16 = _mixer(x, c, w_ada[l], b_ada[l], g_pre_mix[l], g_post_mix[l], w_in[l],
                                                w_pool[l], pool_scale[l], conv_w, l, conv_b[l], w_bout[l], w_o[l],
                                                w_up[l], w_down[l])
        x = _ffn(x, mod, g_pre_ffn[l], g_post_ffn[l], w_up_bf16, ffn_conv_w, l, ffn_conv_b[l], w_down_bf16)
    return x
```

```python
import functools

import jax
import jax.numpy as jnp
from jax import lax
from jax.experimental import pallas as pl
from jax.experimental.pallas import tpu as pltpu

POOL_WINDOWS = (2, 4, 8, 16)
N_MOD = 6
EPS = 1e-6
GELU_K = 0.7978845608028654
GELU_A = 0.044715

V7X_LANES = 128
V7X_SUBLANES = 8
V7X_BF16_SUBLANES = 16
V7X_VMEM_BYTES = 64 * 1024 * 1024
V7X_SCOPED_VMEM_MAX_BYTES = 60000 * 1024

POOL_HALO = 16
CONV_HALO = 8

MIXER_TILE = 512
FFN_TILE = 512
WEIGHT_LOAD_COLS = 512
WEIGHT_LOAD_SLOTS = 4
ROW_SPLIT = 2
FFN_CHUNKS = (1280, 1536)


def _vmem_limit(resident_bytes, streamed_bytes, temp_bytes):
    need = resident_bytes + 2 * streamed_bytes + temp_bytes
    assert need < V7X_VMEM_BYTES, need
    return int(need)


def _const_spec(shape):
    zeros = (0,) * len(shape)
    return pl.BlockSpec(shape, lambda *_: zeros, pipeline_mode=pl.Buffered(1))


def _layer_spec(stacked, layer):
    zeros = (0,) * (stacked.ndim - 1)
    return pl.BlockSpec((1,) + stacked.shape[1:], lambda *_: (layer,) + zeros, pipeline_mode=pl.Buffered(1))


def _rmsnorm(xf, gain_row):
    return xf * lax.rsqrt(jnp.mean(xf * xf, axis=-1, keepdims=True) + EPS) * gain_row


def _put_slabs(buf, halo, val, first):
    for i in range(val.shape[1] // V7X_LANES):
        buf[first + i, halo:, :] = val[:, i * V7X_LANES:(i + 1) * V7X_LANES]


def _rows_back(buf, halo, tile, slabs, k):
    return jnp.concatenate([buf[j, halo - k:halo - k + tile, :] for j in slabs], axis=1)


def _keep_history(buf, halo, tile, slabs):
    for j in slabs:
        buf[j, :halo, :] = buf[j, tile:, :]


def _zero_history(buf, halo):
    buf[:, :halo, :] = jnp.zeros((buf.shape[0], halo, V7X_LANES), jnp.float32)


def _dot(a, b):
    return jnp.dot(a, b, preferred_element_type=jnp.float32)


def _row_blocks(rows, steps):
    for n in range(steps, 0, -1):
        if rows % n == 0 and (rows // n) % V7X_BF16_SUBLANES == 0:
            return n, rows // n
    raise ValueError((rows, steps))


def _mixer_kernel(x_ref, gpre_ref, gpost_ref, convw_ref, convb_ref, c_ref, bada_ref, wpool_ref, pscale_ref,
                  wup_f32, wdown_f32, wada_hbm, win_hbm, wbout_hbm, wo_hbm,
                  o_ref, mod_ref, wup_bf16, wdown_bf16,
                  pool_buf, conv_buf, wfold_ref, win_ref, wbout_ref, wo_ref, stage, stage_sem, *, tile, d):
    b, s = pl.program_id(0), pl.program_id(1)
    n_slabs = d // V7X_LANES
    is_first = (b == 0) & (s == 0)
    cols, n_slots = stage.shape[2], stage.shape[0]
    gw = wpool_ref.shape[1]


    def ada_cols(c0):
        def use(chunk):
            c_bf16 = c_ref[...].astype(jnp.bfloat16)
            mod_ref[:, c0:c0 + cols] = _dot(c_bf16, chunk.astype(jnp.bfloat16)) + bada_ref[:, c0:c0 + cols]
        return use

    def fold_cols(c0):
        def use(chunk):
            for g in range(c0 // gw, (c0 + cols) // gw):
                gcols = slice(g * gw, (g + 1) * gw)
                wp = wpool_ref[g] * (0.5 * pscale_ref[:, gcols])
                wfold = jnp.dot(chunk[:, g * gw - c0:(g + 1) * gw - c0], wp, precision=lax.Precision.HIGHEST,
                                preferred_element_type=jnp.float32)
                wfold_ref[:, gcols] = wfold.astype(jnp.bfloat16)
        return use

    def cast_into(dst, scale=None):
        def use(chunk):
            dst[...] = (chunk if scale is None else chunk * scale).astype(jnp.bfloat16)
        return use

    moves = [(wada_hbm.at[:, pl.ds(c0, cols)], ada_cols(c0)) for c0 in range(0, wada_hbm.shape[1], cols)]
    moves += [(win_hbm.at[:, pl.ds(c0, cols)], fold_cols(c0)) for c0 in range(0, d, cols)]
    n_before_step = len(moves)
    later = {}
    for slab in (3, 1, 2, 4, 5):
        later["w_in slab %d" % slab] = list(range(len(moves), len(moves) + d // cols))
        moves += [(win_hbm.at[:, pl.ds(c0, cols)],
                   cast_into(win_ref.at[:, pl.ds(c0 - d, cols)], 0.5 if slab >= 4 else None))
                  for c0 in range(slab * d, (slab + 1) * d, cols)]
    for name, src, dst in (("w_bout", wbout_hbm, wbout_ref), ("w_o", wo_hbm, wo_ref)):
        later[name] = list(range(len(moves), len(moves) + d // cols))
        moves += [(src.at[:, pl.ds(c0, cols)], cast_into(dst.at[:, pl.ds(c0, cols)])) for c0 in range(0, d, cols)]

    def load(i):
        return pltpu.make_async_copy(moves[i][0], stage.at[i % n_slots], stage_sem.at[i % n_slots])

    def consume(i):
        if i + n_slots - 1 < len(moves):
            load(i + n_slots - 1).start()
        load(i).wait()
        moves[i][1](stage[i % n_slots])


    def step(first):
        order = list(later)
        fetched = []

        def need(name):
            if first:
                upto = min(order.index(name) + 1, len(order) - 1)
                for nxt in order[len(fetched):upto + 1]:
                    for i in later[nxt]:
                        consume(i)
                    fetched.append(nxt)

        wup_bf16[...] = wup_f32[...].astype(jnp.bfloat16)
        wdown_bf16[...] = wdown_f32[...].astype(jnp.bfloat16)

        @pl.when(s == 0)
        def _():
            _zero_history(pool_buf, POOL_HALO)
            _zero_history(conv_buf, CONV_HALO)

        x = x_ref[0]
        sh1, sc1, gt1 = (mod_ref[pl.ds(b, 1), i * d:(i + 1) * d] for i in range(3))
        halves = [slice(i * tile // ROW_SPLIT, (i + 1) * tile // ROW_SPLIT) for i in range(ROW_SPLIT)]
        pre_gain = gpre_ref[...] * (1.0 + sc1)
        hb_parts = [(_rmsnorm(x[rows], pre_gain) + sh1).astype(jnp.bfloat16) for rows in halves]
        hb = jnp.concatenate(hb_parts, axis=0)

        def proj(i):
            need("w_in slab %d" % i)
            return _dot(hb, win_ref[:, (i - 1) * d:i * d])

        u = jnp.concatenate([_dot(part, wfold_ref[...]) for part in hb_parts], axis=0)
        _put_slabs(pool_buf, POOL_HALO, u, 0)
        uc = proj(3)
        v = uc * proj(1)
        _put_slabs(conv_buf, CONV_HALO, v, 0)

        t1 = (s * tile + lax.broadcasted_iota(jnp.int32, (tile, V7X_LANES), 0) + 1).astype(jnp.float32)
        half_ya = []
        for g, w in enumerate(POOL_WINDOWS):
            slabs = range(g * gw // V7X_LANES, (g + 1) * gw // V7X_LANES)
            ug = u[:, g * gw:(g + 1) * gw]
            if w >= V7X_SUBLANES:
                hw, ext = w // 2, V7X_SUBLANES
                part = _rows_back(pool_buf, POOL_HALO - ext, tile + ext, slabs, 0)
                for k in range(1, hw):
                    part = part + _rows_back(pool_buf, POOL_HALO - ext, tile + ext, slabs, k)
                acc = part[ext:] + part[ext - hw:ext - hw + tile]
            else:
                acc = ug
                for k in range(1, w):
                    acc = acc + _rows_back(pool_buf, POOL_HALO, tile, slabs, k)
            inv = 1.0 / jnp.minimum(t1, float(w))
            inv = jnp.concatenate([inv] * len(slabs), axis=1)
            half_ya.append(acc * inv - ug)
        _keep_history(pool_buf, POOL_HALO, tile, range(n_slabs))
        half_ya = jnp.concatenate(half_ya, axis=1)

        ub = proj(2)

        v1 = _rows_back(conv_buf, CONV_HALO, tile, range(n_slabs), 1)
        v2 = _rows_back(conv_buf, CONV_HALO, tile, range(n_slabs), 2)
        _keep_history(conv_buf, CONV_HALO, tile, range(n_slabs))
        cw, cb = 0.5 * convw_ref[0], 0.5 * convb_ref[...]
        half_conv = cb + cw[0:1] * v2 + cw[1:2] * v1 + cw[2:3] * v
        half_ubc = (ub * half_conv).astype(jnp.bfloat16)

        half_za = proj(4)
        half_zb = proj(5)
        gated_a = half_ya + half_ya * jnp.tanh(half_za)
        need("w_bout")
        half_yb = _dot(half_ubc, wbout_ref[...])
        merged = (gated_a + (half_yb + half_yb * jnp.tanh(half_zb))).astype(jnp.bfloat16)
        post_gain = gpost_ref[...] * gt1
        need("w_o")
        for rows in halves:
            o_ref[0, rows, :] = x[rows] + _rmsnorm(_dot(merged[rows], wo_ref[...]), post_gain)

    @pl.when(is_first)
    def _():
        for i in range(n_slots - 1):
            load(i).start()
        for i in range(n_before_step):
            consume(i)
        step(first=True)

    @pl.when(jnp.logical_not(is_first))
    def _():
        step(first=False)


def _mixer(x, c, w_ada, b_ada, g_pre, g_post, w_in, w_pool, pool_scale, conv_w, layer, conv_b, w_bout, w_o, w_up,
           w_down):
    bsz, seq, d = x.shape
    tile = MIXER_TILE
    n_seq = seq // tile
    n_mod = w_ada.shape[1]
    d_rest = w_in.shape[1] - d
    gw = w_pool.shape[1]
    assert d_rest % WEIGHT_LOAD_COLS == 0 and d % WEIGHT_LOAD_COLS == 0 and n_mod % WEIGHT_LOAD_COLS == 0
    assert WEIGHT_LOAD_COLS % gw == 0 and w_ada.shape[0] == d
    weights = (2 * (d * d + d * d_rest + w_bout.size + w_o.size) + WEIGHT_LOAD_SLOTS * d * WEIGHT_LOAD_COLS * 4
               + w_pool.size * 4 + 2 * bsz * n_mod * 4)
    tok = lambda b, s: (b, s, 0)
    hbm = pl.BlockSpec(memory_space=pl.ANY)

    def side_spec(w):
        n_blocks, rows = _row_blocks(w.shape[0], bsz * n_seq)
        return pl.BlockSpec((rows, w.shape[1]), lambda b, s: (jnp.minimum(b * n_seq + s, n_blocks - 1), 0)), rows

    (up_spec, up_rows), (down_spec, down_rows) = side_spec(w_up), side_spec(w_down)
    side_bytes = (up_rows * w_up.shape[1] + down_rows * w_down.shape[1]) * (4 + 2)
    hist_bytes = d * (POOL_HALO + CONV_HALO + 2 * tile) * 4
    return pl.pallas_call(
        functools.partial(_mixer_kernel, tile=tile, d=d),
        out_shape=(jax.ShapeDtypeStruct(x.shape, x.dtype), jax.ShapeDtypeStruct((bsz, n_mod), jnp.float32),
                   jax.ShapeDtypeStruct(w_up.shape, jnp.bfloat16), jax.ShapeDtypeStruct(w_down.shape, jnp.bfloat16)),
        grid=(bsz, n_seq),
        in_specs=[
            pl.BlockSpec((1, tile, d), tok),
            _const_spec((1, d)), _const_spec((1, d)),
            _layer_spec(conv_w, layer), _const_spec((1, d)),
            _const_spec(c.shape), _const_spec((1, n_mod)), _const_spec(w_pool.shape), _const_spec((1, d)),
            up_spec, down_spec,
            hbm, hbm, hbm, hbm,
        ],
        out_specs=(pl.BlockSpec((1, tile, d), tok), pl.BlockSpec((bsz, n_mod), lambda b, s: (0, 0)),
                   up_spec, down_spec),
        scratch_shapes=[pltpu.VMEM((d // V7X_LANES, POOL_HALO + tile, V7X_LANES), jnp.float32),
                        pltpu.VMEM((d // V7X_LANES, CONV_HALO + tile, V7X_LANES), jnp.float32),
                        pltpu.VMEM((d, d), jnp.bfloat16),
                        pltpu.VMEM((d, d_rest), jnp.bfloat16), pltpu.VMEM(w_bout.shape, jnp.bfloat16),
                        pltpu.VMEM(w_o.shape, jnp.bfloat16),
                        pltpu.VMEM((WEIGHT_LOAD_SLOTS, d, WEIGHT_LOAD_COLS), jnp.float32),
                        pltpu.SemaphoreType.DMA((WEIGHT_LOAD_SLOTS,))],
        compiler_params=pltpu.CompilerParams(
            dimension_semantics=("arbitrary", "arbitrary"),
            vmem_limit_bytes=_vmem_limit(weights + hist_bytes, 2 * tile * d * 4 + side_bytes, 8 * tile * d * 4)),
        name="mixer",
    )(x, g_pre.reshape(1, d), g_post.reshape(1, d), conv_w, conv_b.reshape(1, d),
      c, b_ada.reshape(1, n_mod), w_pool, pool_scale.reshape(1, d), w_up, w_down, w_ada, w_in, w_bout, w_o)


def _ffn_kernel(x_ref, mod_ref, gpre_ref, gpost_ref, wup_ref, cw_ref, cb_ref, wdown_ref, o_ref, up_buf,
                *, tile, d, f, chunks):
    s = pl.program_id(1)

    @pl.when(s == 0)
    def _():
        _zero_history(up_buf, CONV_HALO)

    x = x_ref[0]
    sh2, sc2, gt2 = (mod_ref[pl.ds(pl.program_id(0), 1), i * d:(i + 1) * d] for i in range(3, N_MOD))
    hb = (_rmsnorm(x, gpre_ref[...] * (1.0 + sc2)) + sh2).astype(jnp.bfloat16)

    def conv_cols(lo, n, scale=None):
        cols = slice(lo, lo + n)
        up = _dot(hb, wup_ref[:, cols])
        slabs = range(lo // V7X_LANES, (lo + n) // V7X_LANES)
        _put_slabs(up_buf, CONV_HALO, up, slabs[0])
        u1 = _rows_back(up_buf, CONV_HALO, tile, slabs, 1)
        u2 = _rows_back(up_buf, CONV_HALO, tile, slabs, 2)
        _keep_history(up_buf, CONV_HALO, tile, slabs)
        cw, cb = cw_ref[0, :, cols], cb_ref[:, cols]
        if scale is not None:
            cw, cb = cw * scale, cb * scale
        return cb + cw[0:1] * u2 + cw[1:2] * u1 + cw[2:3] * up

    ff = None
    for lo, n in chunks:
        gate = conv_cols(lo, n)
        half_val = conv_cols(f + lo, n, scale=0.5)
        p = gate * half_val
        t = jnp.tanh(gate * (GELU_K + (GELU_K * GELU_A) * (gate * gate)))
        act = (p + p * t).astype(jnp.bfloat16)
        part = _dot(act, wdown_ref[lo:lo + n, :])
        ff = part if ff is None else ff + part
    o_ref[0] = x + _rmsnorm(ff, gpost_ref[...] * gt2)


def _ffn(x, mod, g_pre, g_post, w_up, conv_w, layer, conv_b, w_down):
    bsz, seq, d = x.shape
    tile = FFN_TILE
    f = w_down.shape[0]
    assert sum(FFN_CHUNKS) == f
    chunks = tuple((sum(FFN_CHUNKS[:i]), n) for i, n in enumerate(FFN_CHUNKS))
    weights = 2 * (w_up.size + w_down.size)
    up_buf_bytes = 2 * f * (CONV_HALO + tile) * 4
    tok = pl.BlockSpec((1, tile, d), lambda b, s: (b, s, 0))
    step = functools.partial(_ffn_kernel, tile=tile, d=d, f=f, chunks=chunks)

    def whole_call(x_hbm, mod_ref, gpre_ref, gpost_ref, wup_ref, cw_ref, cb_ref, wdown_ref, o_hbm, up_buf):
        def body(x_ref, o_ref):
            step(x_ref, mod_ref, gpre_ref, gpost_ref, wup_ref, cw_ref.at[pl.ds(layer, 1)], cb_ref, wdown_ref,
                 o_ref, up_buf)
        pltpu.emit_pipeline(body, grid=(bsz, seq // tile), in_specs=[tok], out_specs=[tok])(x_hbm, o_hbm)

    hbm, vmem = pl.BlockSpec(memory_space=pl.ANY), pl.BlockSpec(memory_space=pltpu.VMEM)
    return pl.pallas_call(
        whole_call,
        out_shape=jax.ShapeDtypeStruct(x.shape, x.dtype),
        in_specs=[hbm] + [vmem] * 7,
        out_specs=hbm,
        scratch_shapes=[pltpu.VMEM((2 * f // V7X_LANES, CONV_HALO + tile, V7X_LANES), jnp.float32)],
        compiler_params=pltpu.CompilerParams(
            vmem_limit_bytes=max(_vmem_limit(weights + up_buf_bytes, 2 * tile * d * 4, 4 * tile * d * 4),
                                 V7X_SCOPED_VMEM_MAX_BYTES)),
        name="convffn",
    )(x, mod, g_pre.reshape(1, d), g_post.reshape(1, d), w_up, conv_w, conv_b.reshape(1, 2 * f), w_down)


def kernel(x, c, g_pre_mix, g_post_mix, g_pre_ffn, g_post_ffn, w_ada, b_ada, w_in, w_pool, pool_scale, conv_w,
           conv_b, w_bout, w_o, w_up, ffn_conv_w, ffn_conv_b, w_down):
    for l in range(w_ada.shape[0]):
        x, mod, w_up_bf16, w_down_bf16 = _mixer(x, c, w_ada[l], b_ada[l], g_pre_mix[l], g_post_mix[l], w_in[l],
                                                w_pool[l], pool_scale[l], conv_w, l, conv_b[l], w_bout[l], w_o[l],
                                                w_up[l], w_down[l])
        x = _ffn(x, mod, g_pre_ffn[l], g_post_ffn[l], w_up_bf16, ffn_conv_w, l, ffn_conv_b[l], w_down_bf16)
    return x
```
